```python
import jax, jax.numpy as jnp
from jax import lax
import numpy as np

D_MODEL = 1024
BATCH = 2
SEQ = 16384
DEPTH = 2

N_MIXERS = 2
PLE_DIM = 256
ATT_HEADS = 16
ATT_HEAD_DIM = 64
ATT_WIDTH = ATT_HEADS * ATT_HEAD_DIM
QUERY_BLOCK = 128
REC_HEADS = 8
REC_KEY_DIM = 128
REC_VAL_DIM = 128
REC_WIDTH = REC_HEADS * REC_KEY_DIM
REC_VWIDTH = REC_HEADS * REC_VAL_DIM
CHUNK = 64
N_ATT_LAYERS = (DEPTH + 1) // 2
N_REC_LAYERS = DEPTH // 2
EPS = 1e-6

kernel_name = "fox_hgrn2_interleaved_hybrid"


def rms_norm(x, gain):
    xf = x.astype(jnp.float32)
    y = xf * lax.rsqrt(jnp.mean(xf * xf, axis=-1, keepdims=True) + EPS)
    return (y * gain.astype(jnp.float32)).astype(x.dtype)


def fox_attention(q, k, v, c):
    B, H, S, hd = q.shape
    scale = hd ** -0.5
    kpos = jnp.arange(S)

    def block(i):
        start = i * QUERY_BLOCK
        qb = lax.dynamic_slice_in_dim(q, start, QUERY_BLOCK, axis=2)
        cb = lax.dynamic_slice_in_dim(c, start, QUERY_BLOCK, axis=2)
        s = jnp.einsum('bhqd,bhkd->bhqk', qb, k).astype(jnp.float32) * scale
        s = s + (cb[..., :, None] - c[..., None, :])
        qpos = start + jnp.arange(QUERY_BLOCK)
        s = jnp.where(kpos[None, :] <= qpos[:, None], s, -jnp.inf)
        w = jax.nn.softmax(s, axis=-1)
        return jnp.einsum('bhqk,bhkd->bhqd', w.astype(v.dtype), v)

    out = lax.map(block, jnp.arange(S // QUERY_BLOCK))
    return out.transpose(1, 0, 3, 2, 4).reshape(B, S, H * hd)


def fox_mixer(u, w_in, b_f, w_out):
    B, S, _ = u.shape
    proj = u @ w_in
    q, k, v, g, fl = jnp.split(proj, [ATT_WIDTH, 2 * ATT_WIDTH, 3 * ATT_WIDTH, 4 * ATT_WIDTH], axis=-1)
    heads = lambda t: t.reshape(B, S, ATT_HEADS, ATT_HEAD_DIM).transpose(0, 2, 1, 3)
    log_f = jax.nn.log_sigmoid(fl.astype(jnp.float32) + b_f.astype(jnp.float32))
    c = jnp.cumsum(log_f, axis=1).transpose(0, 2, 1)
    attn = fox_attention(heads(q), heads(k), heads(v), c)
    return (attn * jax.nn.silu(g)) @ w_out


def hgrn2_mixer(u, w_in, lb, out_gain, w_out):
    B, S, _ = u.shape
    proj = u @ w_in
    q, fl, inp, g = jnp.split(proj, [REC_WIDTH, 2 * REC_WIDTH, 2 * REC_WIDTH + REC_VWIDTH], axis=-1)
    lbf = lb.astype(jnp.float32)
    log_f = jnp.logaddexp(jnp.log(lbf), jnp.log1p(-lbf) + jax.nn.log_sigmoid(fl.astype(jnp.float32)))
    k = -jnp.expm1(log_f)

    def chunks(t, dh):
        return t.astype(jnp.float32).reshape(B, S // CHUNK, CHUNK, REC_HEADS, dh).transpose(1, 0, 3, 2, 4)

    qc, kc, gc = chunks(q, REC_KEY_DIM), chunks(k, REC_KEY_DIM), chunks(log_f, REC_KEY_DIM)
    ic = chunks(inp, REC_VAL_DIM)
    causal = jnp.tril(jnp.ones((CHUNK, CHUNK), dtype=bool))

    def step(state, xs):
        qb, kb, gb, ib = xs
        b = jnp.cumsum(gb, axis=-2)
        diff = b[..., :, None, :] - b[..., None, :, :]
        decay = jnp.exp(jnp.where(causal[:, :, None], diff, -jnp.inf))
        scores = jnp.einsum('bhtd,bhsd,bhtsd->bhts', qb, kb, decay)
        o = scores @ ib + jnp.einsum('bhtd,bhdv->bhtv', qb * jnp.exp(b), state)
        b_last = b[..., -1:, :]
        state = jnp.exp(b_last[..., 0, :])[..., None] * state + \
            jnp.einsum('bhsd,bhsv->bhdv', kb * jnp.exp(b_last - b), ib)
        return state, o

    state0 = jnp.zeros((B, REC_HEADS, REC_KEY_DIM, REC_VAL_DIM), jnp.float32)
    _, o = lax.scan(step, state0, (qc, kc, gc, ic))
    o = o.transpose(1, 0, 3, 2, 4).reshape(B, S, REC_HEADS, REC_VAL_DIM)
    o = o * lax.rsqrt(jnp.mean(o * o, axis=-1, keepdims=True) + EPS) * out_gain.astype(jnp.float32)
    gh = g.astype(jnp.float32).reshape(B, S, REC_HEADS, REC_VAL_DIM)
    y = (o * jax.nn.silu(gh)).reshape(B, S, REC_VWIDTH).astype(u.dtype)
    return y @ w_out


def setup_inputs(seed: int = 0) -> dict:
    key = jax.random.key(seed)
    ks = jax.random.split(key, 14)
    nrm = jax.random.normal
    D = D_MODEL
    return {
        "x": nrm(ks[0], (BATCH, SEQ, D), jnp.float32),
        "p": nrm(ks[1], (DEPTH, BATCH, SEQ, PLE_DIM), jnp.float32),
        "norm_pre": 1.0 + 0.05 * nrm(ks[2], (DEPTH, D), jnp.float32),
        "norm_post": 1.0 + 0.05 * nrm(ks[3], (DEPTH, D), jnp.float32),
        "att_w_in": nrm(ks[4], (N_ATT_LAYERS, D, 4 * ATT_WIDTH + ATT_HEADS), jnp.float32) * D ** -0.5,
        "att_b_f": jax.random.uniform(ks[5], (N_ATT_LAYERS, ATT_HEADS), jnp.float32, 1.0, 6.0),
        "att_w_out": nrm(ks[6], (N_ATT_LAYERS, ATT_WIDTH, D), jnp.float32) * ATT_WIDTH ** -0.5,
        "rec_w_in": nrm(ks[7], (N_REC_LAYERS, D, 2 * REC_WIDTH + 2 * REC_VWIDTH), jnp.float32) * D ** -0.5,
        "rec_lb": 1.0 + 0.1 * nrm(ks[8], (DEPTH, REC_WIDTH), jnp.float32),
        "rec_out_norm": 1.0 + 0.05 * nrm(ks[9], (N_REC_LAYERS, REC_VAL_DIM), jnp.float32),
        "rec_w_out": nrm(ks[10], (N_REC_LAYERS, REC_VWIDTH, D), jnp.float32) * REC_VWIDTH ** -0.5,
        "ple_w_proj": nrm(ks[11], (DEPTH, PLE_DIM, D), jnp.float32) * PLE_DIM ** -0.5,
        "ple_w_gate": nrm(ks[12], (DEPTH, D, D), jnp.float32) * D ** -0.5,
    }


def reference(x, p, norm_pre, norm_post, att_w_in, att_b_f, att_w_out, rec_w_in, rec_lb,
              rec_out_norm, rec_w_out, ple_w_proj, ple_w_gate):
    sm = jax.nn.softmax(rec_lb.astype(jnp.float32), axis=0)
    lower_bounds = jnp.cumsum(sm, axis=0) - sm[0:1]
    h = x
    for layer in range(DEPTH):
        u = rms_norm(h, norm_pre[layer])
        j = layer // N_MIXERS
        if layer % N_MIXERS == 0:
            y = fox_mixer(u, att_w_in[j], att_b_f[j], att_w_out[j])
        else:
            y = hgrn2_mixer(u, rec_w_in[j], lower_bounds[layer], rec_out_norm[j], rec_w_out[j])
        h = h + rms_norm(y, norm_post[layer])
        gate = jax.nn.sigmoid((h @ ple_w_gate[layer]).astype(jnp.float32)).astype(h.dtype)
        h = h + (p[layer] @ ple_w_proj[layer]) * gate
    return h
```

```python
import functools

import numpy as np
import jax
import jax.numpy as jnp
from jax import lax
from jax.experimental import pallas as pl
from jax.experimental.pallas import tpu as pltpu

F32 = jnp.float32
BF16 = jnp.bfloat16

D_MODEL = 1024
PLE_DIM = 256
ATT_HEADS = 16
ATT_HEAD_DIM = 64
REC_HEADS = 8
REC_DIM = 128
EPS = 1e-6

LANES = 128
HEAD_PAIRS = ATT_HEADS // 2
C_PIECES = 3

ROW_TILE = 512
ATT_BLOCK = 512
REC_CHUNK = 256
VMEM_LIMIT = 56 * 1024 * 1024

NT_DIMS = (((1,), (1,)), ((), ()))
TN_DIMS = (((0,), (0,)), ((), ()))


def _dot(a, b):
    return jnp.dot(a, b, preferred_element_type=F32)


def _dot_nt(a, b):
    return lax.dot_general(a, b, NT_DIMS, preferred_element_type=F32)


def _dot_tn(a, b):
    return lax.dot_general(a, b, TN_DIMS, preferred_element_type=F32)


def _split_bf16(x, pieces):
    out = []
    r = x
    for _ in range(pieces - 1):
        p = r.astype(BF16)
        out.append(p)
        r = r - p.astype(F32)
    out.append(r.astype(BF16))
    return out


def _rms_norm(x, gain):
    ms = jnp.mean(x * x, axis=-1, keepdims=True)
    return x * lax.rsqrt(ms + EPS) * gain


def _sigmoid(x):
    return 1.0 / (1.0 + jnp.exp(-x))


def _fox_proj_kernel(x_ref, gain_ref, wq_ref, wk_ref, wvt_ref, wg_ref, wfl_ref, bf_ref,
                     tril_ref, sel_ref, q_ref, k_ref, vt_ref, gs_ref, caug_ref, carry_ref):
    tm = x_ref.shape[1]
    u = _rms_norm(x_ref[0], gain_ref[...]).astype(BF16)
    q_ref[0] = (_dot(u, wq_ref[...]) * (ATT_HEAD_DIM ** -0.5)).astype(BF16)
    k_ref[0] = _dot(u, wk_ref[...]).astype(BF16)
    vt_ref[0, 0] = _dot_nt(wvt_ref[...], u).astype(BF16)
    g = _dot(u, wg_ref[...])
    gs_ref[0] = (g * _sigmoid(g)).astype(BF16)

    fl = _dot(u, wfl_ref[...]) + bf_ref[...]
    log_f = jnp.minimum(fl, 0.0) - jnp.log1p(jnp.exp(-jnp.abs(fl)))
    tril = tril_ref[...]
    c_blk = sum(_dot(tril, piece) for piece in _split_bf16(log_f, C_PIECES))

    @pl.when(pl.program_id(1) == 0)
    def _():
        carry_ref[...] = jnp.zeros_like(carry_ref)

    c = c_blk + carry_ref[...]
    carry_ref[...] = c[tm - 1:tm, :]
    caug = sum(_dot(piece, sel_ref[i]) for i, piece in enumerate(_split_bf16(-c, C_PIECES)))
    caug_ref[0] = caug.astype(BF16)


def _fox_proj(x, gain, wq, wk, wvt, wg, wfl, bf, tril, sel):
    B, S, D = x.shape
    tm = ROW_TILE
    nb = S // tm
    const = lambda shape: pl.BlockSpec(shape, lambda b, i: (0,) * len(shape))
    tok = pl.BlockSpec((1, tm, D), lambda b, i: (b, i, 0))
    out_tok = jax.ShapeDtypeStruct((B, S, D), BF16)
    return pl.pallas_call(
        _fox_proj_kernel,
        grid=(B, nb),
        in_specs=[tok, const((1, D)), const((D, D)), const((D, D)), const((D, D)), const((D, D)),
                  const((D, LANES)), const((1, LANES)), const((tm, tm)), const((C_PIECES, LANES, D))],
        out_specs=[tok, tok, pl.BlockSpec((1, 1, D, tm), lambda b, i: (b, i, 0, 0)), tok, tok],
        out_shape=[out_tok, out_tok, jax.ShapeDtypeStruct((B, nb, D, tm), BF16), out_tok, out_tok],
        scratch_shapes=[pltpu.VMEM((1, LANES), F32)],
        compiler_params=pltpu.CompilerParams(
            dimension_semantics=("arbitrary", "arbitrary"), vmem_limit_bytes=VMEM_LIMIT),
    )(x, gain, wq, wk, wvt, wg, wfl, bf, tril, sel)


def _fox_attn_kernel(q_ref, k_ref, caug_ref, vt_ref, gs_ref, z_ref):
    bq = q_ref.shape[1]
    bk = ATT_BLOCK
    i = pl.program_id(2)
    q = q_ref[0]
    lane = lax.broadcasted_iota(jnp.int32, (bq, LANES), 1)
    key_pos = lax.broadcasted_iota(jnp.int32, (bk, bq), 0)
    qry_pos = lax.broadcasted_iota(jnp.int32, (bk, bq), 1)
    ones_rows = jnp.ones((16, bk), BF16)

    outs = []
    for hh in range(2):
        lo = hh * ATT_HEAD_DIM
        q_head = jnp.where((lane >= lo) & (lane < lo + ATT_HEAD_DIM), q, jnp.zeros_like(q))
        pick = jnp.where((lane >= hh * C_PIECES) & (lane < (hh + 1) * C_PIECES), 1.0, 0.0).astype(BF16)
        q_aug = jnp.concatenate([q_head, pick], axis=1)

        def block(j, m, acc, diagonal):
            rows = pl.ds(pl.multiple_of(j * bk, bk), bk)
            k_aug = jnp.concatenate([k_ref[0, rows, :], caug_ref[0, rows, :]], axis=1)
            s_t = _dot_nt(k_aug, q_aug)
            if diagonal:
                s_t = jnp.where(key_pos <= qry_pos, s_t, -jnp.inf)
            m_new = jnp.maximum(m, jnp.max(s_t, axis=0, keepdims=True))
            alpha = jnp.exp(m - m_new)
            p_t = jnp.exp(s_t - m_new).astype(BF16)
            v_aug = jnp.concatenate([vt_ref[0, j, lo:lo + ATT_HEAD_DIM, :], ones_rows], axis=0)
            acc = alpha * acc + _dot(v_aug, p_t)
            return m_new, acc

        m0 = jnp.full((1, bq), -jnp.inf, F32)
        acc0 = jnp.zeros((ATT_HEAD_DIM + 16, bq), F32)
        m, acc = block(i, m0, acc0, True)
        m, acc = lax.fori_loop(0, i, lambda j, c: block(j, c[0], c[1], False), (m, acc))
        denom = acc[ATT_HEAD_DIM:ATT_HEAD_DIM + 1, :]
        outs.append(acc[:ATT_HEAD_DIM, :] / denom)

    o = jnp.concatenate(outs, axis=0).T
    z_ref[0] = (o * gs_ref[0].astype(F32)).astype(BF16)


def _fox_attn(q, k, caug, vt, gs):
    B, S, D = q.shape
    bq = ATT_BLOCK
    nb = S // bq
    tok = pl.BlockSpec((1, bq, LANES), lambda b, p, i: (b, i, p))
    seq = pl.BlockSpec((1, S, LANES), lambda b, p, i: (b, 0, p))
    return pl.pallas_call(
        _fox_attn_kernel,
        grid=(B, HEAD_PAIRS, nb),
        in_specs=[tok, seq, seq,
                  pl.BlockSpec((1, vt.shape[1], LANES, vt.shape[3]), lambda b, p, i: (b, 0, p, 0)),
                  tok],
        out_specs=tok,
        out_shape=jax.ShapeDtypeStruct((B, S, D), BF16),
        compiler_params=pltpu.CompilerParams(
            dimension_semantics=("arbitrary", "arbitrary", "arbitrary"), vmem_limit_bytes=VMEM_LIMIT),
    )(q, k, caug, vt, gs)


def _post_kernel(z_ref, h_ref, p_ref, wout_ref, gain_ref, wgate_ref, wproj_ref, o_ref):
    y = _dot(z_ref[...], wout_ref[...])
    h1 = h_ref[...] + _rms_norm(y, gain_ref[...])
    gate = _sigmoid(_dot(h1.astype(BF16), wgate_ref[...]))
    o_ref[...] = h1 + _dot(p_ref[...].astype(BF16), wproj_ref[...]) * gate


def _post(z, h, p, wout, gain, wgate, wproj):
    T, D = h.shape
    tm = ROW_TILE
    const = lambda shape: pl.BlockSpec(shape, lambda i: (0,) * len(shape))
    tok = lambda w: pl.BlockSpec((tm, w), lambda i: (i, 0))
    return pl.pallas_call(
        _post_kernel,
        grid=(T // tm,),
        in_specs=[tok(D), tok(D), tok(PLE_DIM), const((D, D)), const((1, D)), const((D, D)),
                  const((PLE_DIM, D))],
        out_specs=tok(D),
        out_shape=jax.ShapeDtypeStruct((T, D), F32),
        compiler_params=pltpu.CompilerParams(
            dimension_semantics=("arbitrary",), vmem_limit_bytes=VMEM_LIMIT),
    )(z, h, p, wout, gain, wgate, wproj)


def _rec_proj_kernel(h_ref, gain_ref, wq_ref, wf_ref, wi_ref, wg_ref, q_ref, fl_ref, v_ref, gs_ref):
    u = _rms_norm(h_ref[...], gain_ref[...]).astype(BF16)
    q_ref[...] = _dot(u, wq_ref[...]).astype(BF16)
    fl_ref[...] = _dot(u, wf_ref[...])
    v_ref[...] = _dot(u, wi_ref[...]).astype(BF16)
    g = _dot(u, wg_ref[...])
    gs_ref[...] = (g * _sigmoid(g)).astype(BF16)


def _rec_proj(h, gain, wq, wf, wi, wg):
    T, D = h.shape
    tm = ROW_TILE
    const = lambda shape: pl.BlockSpec(shape, lambda i: (0,) * len(shape))
    tok = pl.BlockSpec((tm, D), lambda i: (i, 0))
    bf = jax.ShapeDtypeStruct((T, D), BF16)
    return pl.pallas_call(
        _rec_proj_kernel,
        grid=(T // tm,),
        in_specs=[tok, const((1, D)), const((D, D)), const((D, D)), const((D, D)), const((D, D))],
        out_specs=[tok, tok, tok, tok],
        out_shape=[bf, jax.ShapeDtypeStruct((T, D), F32), bf, bf],
        compiler_params=pltpu.CompilerParams(
            dimension_semantics=("arbitrary",), vmem_limit_bytes=VMEM_LIMIT),
    )(h, gain, wq, wf, wi, wg)


def _rec_levels():
    h = REC_CHUNK // 2
    out = []
    while h >= 1:
        out.append(h)
        h //= 2
    return out


def _rec_kernel(q_ref, fl_ref, v_ref, gs_ref, lb_ref, gain_ref, tril_ref, lvl_ref, z_ref, st_ref):
    C = REC_CHUNK

    @pl.when(pl.program_id(2) == 0)
    def _():
        st_ref[...] = jnp.zeros_like(st_ref)

    lbs = lb_ref[...]
    e = jnp.exp(lbs - jnp.max(lbs, axis=0, keepdims=True))
    sm = e / jnp.sum(e, axis=0, keepdims=True)
    lb = (sm[0:1] + sm[1:2]) - sm[0:1]

    fl = fl_ref[0]
    log_f = jnp.log(lb + (1.0 - lb) * _sigmoid(fl))
    kk = (1.0 - lb) * _sigmoid(-fl)
    q = q_ref[0].astype(F32)
    v = v_ref[0]
    tril = tril_ref[...]
    b = sum(_dot(tril, piece) for piece in _split_bf16(log_f, C_PIECES))

    row = lax.broadcasted_iota(jnp.int32, (C, LANES), 0)
    lvl = lvl_ref[...]
    a = jnp.where(lvl == 0, _dot_nt(q.astype(BF16), kk.astype(BF16)), 0.0)
    b_m1 = pltpu.roll(b, 1, 0)
    for n, h in enumerate(_rec_levels()):
        if h >= 4:
            b3 = b.reshape(C // (2 * h), 2 * h, LANES)
            ref = jnp.broadcast_to(b3[:, h - 1:h, :], b3.shape).reshape(C, LANES)
        elif h == 2:
            o4 = row & 3
            ref = jnp.where(o4 == 0, pltpu.roll(b, C - 1, 0),
                            jnp.where(o4 == 1, b, jnp.where(o4 == 2, b_m1, pltpu.roll(b, 2, 0))))
        else:
            ref = jnp.where((row & 1) == 0, b, b_m1)
        upper = (row & h) != 0
        d = b - ref
        w = jnp.exp(jnp.where(upper, d, -d))
        q_l = jnp.where(upper, q * w, 0.0).astype(BF16)
        k_l = jnp.where(upper, 0.0, kk * w).astype(BF16)
        a = jnp.where(lvl == n + 1, _dot_nt(q_l, k_l), a)

    st = st_ref[...]
    o = _dot(a.astype(BF16), v) + _dot_nt((q * jnp.exp(b)).astype(BF16), st.astype(BF16))
    b_last = b[C - 1:C, :]
    k_dec = (kk * jnp.exp(b_last - b)).astype(BF16)
    st_ref[...] = st * jnp.exp(b_last) + _dot_tn(v, k_dec)

    o = o * lax.rsqrt(jnp.mean(o * o, axis=-1, keepdims=True) + EPS) * gain_ref[...]
    z_ref[0] = (o * gs_ref[0].astype(F32)).astype(BF16)


def _rec_level_map():
    t = np.arange(REC_CHUNK)[:, None]
    s = np.arange(REC_CHUNK)[None, :]
    out = np.full((REC_CHUNK, REC_CHUNK), -1, np.int32)
    out[t == s] = 0
    for n, h in enumerate(_rec_levels()):
        own = (t // (2 * h) == s // (2 * h)) & ((t & h) != 0) & ((s & h) == 0)
        out[own] = n + 1
    return out


def _rec(q, fl, v, gs, rec_lb, gain, tril, lvl):
    B, S, D = q.shape
    C = REC_CHUNK
    tok = pl.BlockSpec((1, C, REC_DIM), lambda b, h, i: (b, i, h))
    const = lambda shape: pl.BlockSpec(shape, lambda b, h, i: (0,) * len(shape))
    return pl.pallas_call(
        _rec_kernel,
        grid=(B, REC_HEADS, S // C),
        in_specs=[tok, tok, tok, tok,
                  pl.BlockSpec((rec_lb.shape[0], REC_DIM), lambda b, h, i: (0, h)),
                  const((1, REC_DIM)), const((C, C)), const((C, C))],
        out_specs=tok,
        out_shape=jax.ShapeDtypeStruct((B, S, D), BF16),
        scratch_shapes=[pltpu.VMEM((REC_DIM, REC_DIM), F32)],
        compiler_params=pltpu.CompilerParams(
            dimension_semantics=("arbitrary", "arbitrary", "arbitrary"), vmem_limit_bytes=VMEM_LIMIT),
    )(q, fl, v, gs, rec_lb, gain, tril, lvl)


def _bias_selector():
    sel = np.zeros((C_PIECES, LANES, D_MODEL), np.float32)
    for head in range(ATT_HEADS):
        pair, hh = divmod(head, 2)
        for i in range(C_PIECES):
            sel[i, head, pair * LANES + hh * C_PIECES + i] = 1.0
    return sel


def kernel(x, p, norm_pre, norm_post, att_w_in, att_b_f, att_w_out, rec_w_in, rec_lb, rec_out_norm,
           rec_w_out, ple_w_proj, ple_w_gate):
    B, S, D = x.shape
    assert D == D_MODEL and S % ROW_TILE == 0 and S % ATT_BLOCK == 0 and S % REC_CHUNK == 0
    assert norm_pre.shape[0] == 2 and rec_lb.shape[0] == 2
    T = B * S
    W = ATT_HEADS * ATT_HEAD_DIM

    w_in = att_w_in[0]
    wq, wk, wv, wg = (w_in[:, n * W:(n + 1) * W].astype(BF16) for n in range(4))
    wfl = jnp.pad(w_in[:, 4 * W:], ((0, 0), (0, LANES - ATT_HEADS))).astype(BF16)
    bf = jnp.pad(att_b_f[0], (0, LANES - ATT_HEADS)).reshape(1, LANES)
    tril_row = jnp.asarray(np.tril(np.ones((ROW_TILE, ROW_TILE), np.float32)), BF16)
    sel = jnp.asarray(_bias_selector(), BF16)
    q, k, vt, gs, caug = _fox_proj(x, norm_pre[0:1], wq, wk, wv.T, wg, wfl, bf, tril_row, sel)
    z = _fox_attn(q, k, caug, vt, gs)
    h = _post(z.reshape(T, D), x.reshape(T, D), p[0].reshape(T, PLE_DIM), att_w_out[0].astype(BF16),
              norm_post[0:1], ple_w_gate[0].astype(BF16), ple_w_proj[0].astype(BF16))

    r_in = rec_w_in[0]
    RW = REC_HEADS * REC_DIM
    rq, rf, ri, rg = (r_in[:, n * RW:(n + 1) * RW].astype(BF16) for n in range(4))
    q1, fl1, v1, gs1 = _rec_proj(h, norm_pre[1:2], rq, rf, ri, rg)
    tril_rec = jnp.asarray(np.tril(np.ones((REC_CHUNK, REC_CHUNK), np.float32)), BF16)
    lvl = jnp.asarray(_rec_level_map())
    shape3 = lambda a: a.reshape(B, S, D)
    z1 = _rec(shape3(q1), shape3(fl1), shape3(v1), shape3(gs1), rec_lb, rec_out_norm[0:1],
              tril_rec, lvl)
    out = _post(z1.reshape(T, D), h, p[1].reshape(T, PLE_DIM), rec_w_out[0].astype(BF16),
                norm_post[1:2], ple_w_gate[1].astype(BF16), ple_w_proj[1].astype(BF16))
    return out.reshape(B, S, D)
```

```python
import functools

import numpy as np
import jax
import jax.numpy as jnp
from jax import lax
from jax.experimental import pallas as pl
from jax.experimental.pallas import tpu as pltpu

F32 = jnp.float32
BF16 = jnp.bfloat16

D_MODEL = 1024
PLE_DIM = 256
ATT_HEADS = 16
ATT_HEAD_DIM = 64
REC_HEADS = 8
REC_DIM = 128
EPS = 1e-6
LOG2_E = 1.4426950408889634
SKIP_LOG2 = 130.0
NORM_MARGIN = 1.01

LANES = 128
HEAD_PAIRS = ATT_HEADS // 2
C_PIECES = 3

ROW_TILE = 512
ATT_BLOCK = 512
REC_CHUNK = 256
VMEM_LIMIT = 56 * 1024 * 1024

NT_DIMS = (((1,), (1,)), ((), ()))
TN_DIMS = (((0,), (0,)), ((), ()))


def _dot(a, b):
    return jnp.dot(a, b, preferred_element_type=F32)


def _dot_nt(a, b):
    return lax.dot_general(a, b, NT_DIMS, preferred_element_type=F32)


def _dot_tn(a, b):
    return lax.dot_general(a, b, TN_DIMS, preferred_element_type=F32)


def _split_bf16(x, pieces):
    out = []
    r = x
    for _ in range(pieces - 1):
        p = r.astype(BF16)
        out.append(p)
        r = r - p.astype(F32)
    out.append(r.astype(BF16))
    return out


def _rms_norm(x, gain):
    ms = jnp.mean(x * x, axis=-1, keepdims=True)
    return x * lax.rsqrt(ms + EPS) * gain


def _sigmoid(x):
    return 1.0 / (1.0 + jnp.exp(-x))


def _fox_proj_kernel(x_ref, gain_ref, wq_ref, wk_ref, wvt_ref, wg_ref, wfl_ref, bf_ref,
                     tril_ref, sel_ref, hsel_ref, q_ref, k_ref, vt_ref, gs_ref, caug_ref, nblk_ref,
                     carry_ref, kpre_run_ref, kpre_tab_ref, clast_tab_ref):
    tm = x_ref.shape[1]
    i = pl.program_id(1)
    u = _rms_norm(x_ref[0], gain_ref[...]).astype(BF16)
    qb = (_dot(u, wq_ref[...]) * (ATT_HEAD_DIM ** -0.5 * LOG2_E)).astype(BF16)
    kb = _dot(u, wk_ref[...]).astype(BF16)
    q_ref[0] = qb
    k_ref[0] = kb
    vt_ref[0, 0] = _dot_nt(wvt_ref[...], u).astype(BF16)
    g = _dot(u, wg_ref[...])
    gs_ref[0] = (g * _sigmoid(g)).astype(BF16)

    fl = _dot(u, wfl_ref[...]) + bf_ref[...]
    log_f = jnp.minimum(fl, 0.0) - jnp.log1p(jnp.exp(-jnp.abs(fl)))
    tril = tril_ref[...]
    c_blk = sum(_dot(tril, piece) for piece in _split_bf16(log_f, C_PIECES))

    @pl.when(i == 0)
    def _():
        carry_ref[...] = jnp.zeros_like(carry_ref)
        kpre_run_ref[...] = jnp.zeros_like(kpre_run_ref)
        kpre_tab_ref[...] = jnp.zeros_like(kpre_tab_ref)
        clast_tab_ref[...] = jnp.zeros_like(clast_tab_ref)

    c = c_blk + carry_ref[...]
    carry_ref[...] = c[tm - 1:tm, :]
    c2 = c * LOG2_E
    caug = sum(_dot(piece, sel_ref[n]) for n, piece in enumerate(_split_bf16(-c2, C_PIECES)))
    caug_ref[0] = caug.astype(BF16)

    def head_norm_max(t):
        tf = t.astype(F32)
        n2 = _dot((tf * tf).astype(BF16), hsel_ref[...])
        return jnp.sqrt(jnp.max(n2, axis=0, keepdims=True)) * NORM_MARGIN

    qmax = head_norm_max(qb)
    kmax = head_norm_max(kb)
    kpre = jnp.maximum(kpre_run_ref[...], kmax)
    kpre_run_ref[...] = kpre
    kpre_tab_ref[pl.ds(i, 1), :] = kpre
    clast_tab_ref[pl.ds(i, 1), :] = c2[tm - 1:tm, :]
    blk = lax.broadcasted_iota(jnp.int32, kpre_tab_ref.shape, 0)
    bound = qmax * kpre_tab_ref[...] - clast_tab_ref[...] + (qmax * kmax + c2[0:1, :])
    first_kept = jnp.min(jnp.where((blk < i) & (bound >= -SKIP_LOG2), blk, i), axis=0, keepdims=True)
    nblk_ref[0, 0] = jnp.broadcast_to(i - first_kept + 1, nblk_ref.shape[2:])


def _fox_proj(x, gain, wq, wk, wvt, wg, wfl, bf, tril, sel, hsel):
    B, S, D = x.shape
    tm = ROW_TILE
    nb = S // tm
    const = lambda shape: pl.BlockSpec(shape, lambda b, i: (0,) * len(shape))
    tok = pl.BlockSpec((1, tm, D), lambda b, i: (b, i, 0))
    out_tok = jax.ShapeDtypeStruct((B, S, D), BF16)
    return pl.pallas_call(
        _fox_proj_kernel,
        grid=(B, nb),
        in_specs=[tok, const((1, D)), const((D, D)), const((D, D)), const((D, D)), const((D, D)),
                  const((D, LANES)), const((1, LANES)), const((tm, tm)), const((C_PIECES, LANES, D)),
                  const((D, LANES))],
        out_specs=[tok, tok, pl.BlockSpec((1, 1, D, tm), lambda b, i: (b, i, 0, 0)), tok, tok,
                   pl.BlockSpec((1, 1, 8, LANES), lambda b, i: (b, i, 0, 0))],
        out_shape=[out_tok, out_tok, jax.ShapeDtypeStruct((B, nb, D, tm), BF16), out_tok, out_tok,
                   jax.ShapeDtypeStruct((B, nb, 8, LANES), jnp.int32)],
        scratch_shapes=[pltpu.VMEM((1, LANES), F32), pltpu.VMEM((1, LANES), F32),
                        pltpu.VMEM((nb, LANES), F32), pltpu.VMEM((nb, LANES), F32)],
        compiler_params=pltpu.CompilerParams(
            dimension_semantics=("arbitrary", "arbitrary"), vmem_limit_bytes=VMEM_LIMIT),
    )(x, gain, wq, wk, wvt, wg, wfl, bf, tril, sel, hsel)


def _fox_attn_kernel(nblk_ref, q_ref, k_ref, caug_ref, vt_ref, gs_ref, z_ref,
                     s_even, s_odd, m_scr, acc_scr):
    bq = q_ref.shape[1]
    bk = ATT_BLOCK
    b, pair, i = pl.program_id(0), pl.program_id(1), pl.program_id(2)
    nb = pl.num_programs(2)
    q = q_ref[0]
    lane = lax.broadcasted_iota(jnp.int32, (bq, LANES), 1)
    ones_rows = jnp.ones((16, bk), BF16)

    outs = []
    for hh in range(2):
        lo = hh * ATT_HEAD_DIM
        q_head = jnp.where((lane >= lo) & (lane < lo + ATT_HEAD_DIM), q, jnp.zeros_like(q))
        pick = jnp.where((lane >= hh * C_PIECES) & (lane < (hh + 1) * C_PIECES), 1.0, 0.0).astype(BF16)
        q_aug = jnp.concatenate([q_head, pick], axis=1)

        def put_scores(j, dst, diagonal=False):
            rows = pl.ds(pl.multiple_of(j * bk, bk), bk)
            k_aug = jnp.concatenate([k_ref[0, rows, :], caug_ref[0, rows, :]], axis=1)
            s_t = _dot_nt(k_aug, q_aug)
            if diagonal:
                key_pos = lax.broadcasted_iota(jnp.int32, (bk, bq), 0)
                qry_pos = lax.broadcasted_iota(jnp.int32, (bk, bq), 1)
                s_t = jnp.where(key_pos <= qry_pos, s_t, -jnp.inf)
            dst[...] = s_t

        def update(j, src):
            m = m_scr[...]
            s_t = src[...]
            m_new = jnp.maximum(m, jnp.max(s_t, axis=0, keepdims=True))
            alpha = jnp.exp2(m - m_new)
            p_t = jnp.exp2(s_t - m_new).astype(BF16)
            v_aug = jnp.concatenate([vt_ref[0, j, lo:lo + ATT_HEAD_DIM, :], ones_rows], axis=0)
            acc_scr[...] = alpha * acc_scr[...] + _dot(v_aug, p_t)
            m_scr[...] = m_new

        n_steps = nblk_ref[(b * ATT_HEADS + 2 * pair + hh) * nb + i]
        m_scr[...] = jnp.full(m_scr.shape, -jnp.inf, F32)
        acc_scr[...] = jnp.zeros(acc_scr.shape, F32)
        put_scores(i, s_even, diagonal=True)

        def two_steps(u, carry):
            j = i - 2 * u
            put_scores(jnp.maximum(j - 1, 0), s_odd)
            update(j, s_even)
            put_scores(jnp.maximum(j - 2, 0), s_even)
            update(j - 1, s_odd)
            return carry

        lax.fori_loop(0, n_steps // 2, two_steps, 0)

        @pl.when(n_steps % 2 == 1)
        def _():
            update(i - (n_steps - 1), s_even)

        acc = acc_scr[...]
        outs.append(acc[:ATT_HEAD_DIM, :] / acc[ATT_HEAD_DIM:ATT_HEAD_DIM + 1, :])

    o = jnp.concatenate(outs, axis=0).T
    z_ref[0] = (o * gs_ref[0].astype(F32)).astype(BF16)


def _fox_attn(nblk, q, k, caug, vt, gs):
    B, S, D = q.shape
    bq = ATT_BLOCK
    nb = S // bq
    tok = pl.BlockSpec((1, bq, LANES), lambda b, p, i, n: (b, i, p))
    seq = pl.BlockSpec((1, S, LANES), lambda b, p, i, n: (b, 0, p))
    return pl.pallas_call(
        _fox_attn_kernel,
        grid_spec=pltpu.PrefetchScalarGridSpec(
            num_scalar_prefetch=1,
            grid=(B, HEAD_PAIRS, nb),
            in_specs=[tok, seq, seq,
                      pl.BlockSpec((1, vt.shape[1], LANES, vt.shape[3]), lambda b, p, i, n: (b, 0, p, 0)),
                      tok],
            out_specs=tok,
            scratch_shapes=[pltpu.VMEM((ATT_BLOCK, bq), F32), pltpu.VMEM((ATT_BLOCK, bq), F32),
                            pltpu.VMEM((1, bq), F32), pltpu.VMEM((ATT_HEAD_DIM + 16, bq), F32)]),
        out_shape=jax.ShapeDtypeStruct((B, S, D), BF16),
        compiler_params=pltpu.CompilerParams(
            dimension_semantics=("arbitrary", "arbitrary", "arbitrary"), vmem_limit_bytes=VMEM_LIMIT),
    )(nblk, q, k, caug, vt, gs)


def _post_kernel(z_ref, h_ref, p_ref, wout_ref, gain_ref, wgate_ref, wproj_ref, o_ref):
    y = _dot(z_ref[...], wout_ref[...])
    h1 = h_ref[...] + _rms_norm(y, gain_ref[...])
    gate = _sigmoid(_dot(h1.astype(BF16), wgate_ref[...]))
    o_ref[...] = h1 + _dot(p_ref[...].astype(BF16), wproj_ref[...]) * gate


def _post(z, h, p, wout, gain, wgate, wproj):
    T, D = h.shape
    tm = ROW_TILE
    const = lambda shape: pl.BlockSpec(shape, lambda i: (0,) * len(shape))
    tok = lambda w: pl.BlockSpec((tm, w), lambda i: (i, 0))
    return pl.pallas_call(
        _post_kernel,
        grid=(T // tm,),
        in_specs=[tok(D), tok(D), tok(PLE_DIM), const((D, D)), const((1, D)), const((D, D)),
                  const((PLE_DIM, D))],
        out_specs=tok(D),
        out_shape=jax.ShapeDtypeStruct((T, D), F32),
        compiler_params=pltpu.CompilerParams(
            dimension_semantics=("arbitrary",), vmem_limit_bytes=VMEM_LIMIT),
    )(z, h, p, wout, gain, wgate, wproj)


def _rec_proj_kernel(h_ref, gain_ref, wq_ref, wf_ref, wi_ref, wg_ref, q_ref, fl_ref, v_ref, gs_ref):
    u = _rms_norm(h_ref[...], gain_ref[...]).astype(BF16)
    q_ref[...] = _dot(u, wq_ref[...]).astype(BF16)
    fl_ref[...] = _dot(u, wf_ref[...])
    v_ref[...] = _dot(u, wi_ref[...]).astype(BF16)
    g = _dot(u, wg_ref[...])
    gs_ref[...] = (g * _sigmoid(g)).astype(BF16)


def _rec_proj(h, gain, wq, wf, wi, wg):
    T, D = h.shape
    tm = ROW_TILE
    const = lambda shape: pl.BlockSpec(shape, lambda i: (0,) * len(shape))
    tok = pl.BlockSpec((tm, D), lambda i: (i, 0))
    bf = jax.ShapeDtypeStruct((T, D), BF16)
    return pl.pallas_call(
        _rec_proj_kernel,
        grid=(T // tm,),
        in_specs=[tok, const((1, D)), const((D, D)), const((D, D)), const((D, D)), const((D, D))],
        out_specs=[tok, tok, tok, tok],
        out_shape=[bf, jax.ShapeDtypeStruct((T, D), F32), bf, bf],
        compiler_params=pltpu.CompilerParams(
            dimension_semantics=("arbitrary",), vmem_limit_bytes=VMEM_LIMIT),
    )(h, gain, wq, wf, wi, wg)


def _rec_levels():
    h = REC_CHUNK // 2
    out = []
    while h >= 1:
        out.append(h)
        h //= 2
    return out


def _rec_kernel(q_ref, fl_ref, v_ref, gs_ref, lb_ref, gain_ref, tril_ref, lvl_ref, z_ref, st_ref):
    C = REC_CHUNK

    @pl.when(pl.program_id(2) == 0)
    def _():
        st_ref[...] = jnp.zeros_like(st_ref)

    lbs = lb_ref[...]
    e = jnp.exp(lbs - jnp.max(lbs, axis=0, keepdims=True))
    sm = e / jnp.sum(e, axis=0, keepdims=True)
    lb = (sm[0:1] + sm[1:2]) - sm[0:1]

    fl = fl_ref[0]
    log_f = jnp.log(lb + (1.0 - lb) * _sigmoid(fl))
    kk = (1.0 - lb) * _sigmoid(-fl)
    q = q_ref[0].astype(F32)
    v = v_ref[0]
    tril = tril_ref[...]
    b = sum(_dot(tril, piece) for piece in _split_bf16(log_f, C_PIECES))

    row = lax.broadcasted_iota(jnp.int32, (C, LANES), 0)
    lvl = lvl_ref[...]
    a = jnp.where(lvl == 0, _dot_nt(q.astype(BF16), kk.astype(BF16)), 0.0)
    b_m1 = pltpu.roll(b, 1, 0)
    for n, h in enumerate(_rec_levels()):
        if h >= 4:
            b3 = b.reshape(C // (2 * h), 2 * h, LANES)
            ref = jnp.broadcast_to(b3[:, h - 1:h, :], b3.shape).reshape(C, LANES)
        elif h == 2:
            o4 = row & 3
            ref = jnp.where(o4 == 0, pltpu.roll(b, C - 1, 0),
                            jnp.where(o4 == 1, b, jnp.where(o4 == 2, b_m1, pltpu.roll(b, 2, 0))))
        else:
            ref = jnp.where((row & 1) == 0, b, b_m1)
        upper = (row & h) != 0
        d = b - ref
        w = jnp.exp(jnp.where(upper, d, -d))
        q_l = jnp.where(upper, q * w, 0.0).astype(BF16)
        k_l = jnp.where(upper, 0.0, kk * w).astype(BF16)
        a = jnp.where(lvl == n + 1, _dot_nt(q_l, k_l), a)

    st = st_ref[...]
    o = _dot(a.astype(BF16), v) + _dot_nt((q * jnp.exp(b)).astype(BF16), st.astype(BF16))
    b_last = b[C - 1:C, :]
    k_dec = (kk * jnp.exp(b_last - b)).astype(BF16)
    st_ref[...] = st * jnp.exp(b_last) + _dot_tn(v, k_dec)

    o = o * lax.rsqrt(jnp.mean(o * o, axis=-1, keepdims=True) + EPS) * gain_ref[...]
    z_ref[0] = (o * gs_ref[0].astype(F32)).astype(BF16)


def _rec_level_map():
    t = np.arange(REC_CHUNK)[:, None]
    s = np.arange(REC_CHUNK)[None, :]
    out = np.full((REC_CHUNK, REC_CHUNK), -1, np.int32)
    out[t == s] = 0
    for n, h in enumerate(_rec_levels()):
        own = (t // (2 * h) == s // (2 * h)) & ((t & h) != 0) & ((s & h) == 0)
        out[own] = n + 1
    return out


def _rec(q, fl, v, gs, rec_lb, gain, tril, lvl):
    B, S, D = q.shape
    C = REC_CHUNK
    tok = pl.BlockSpec((1, C, REC_DIM), lambda b, h, i: (b, i, h))
    const = lambda shape: pl.BlockSpec(shape, lambda b, h, i: (0,) * len(shape))
    return pl.pallas_call(
        _rec_kernel,
        grid=(B, REC_HEADS, S // C),
        in_specs=[tok, tok, tok, tok,
                  pl.BlockSpec((rec_lb.shape[0], REC_DIM), lambda b, h, i: (0, h)),
                  const((1, REC_DIM)), const((C, C)), const((C, C))],
        out_specs=tok,
        out_shape=jax.ShapeDtypeStruct((B, S, D), BF16),
        scratch_shapes=[pltpu.VMEM((REC_DIM, REC_DIM), F32)],
        compiler_params=pltpu.CompilerParams(
            dimension_semantics=("arbitrary", "arbitrary", "arbitrary"), vmem_limit_bytes=VMEM_LIMIT),
    )(q, fl, v, gs, rec_lb, gain, tril, lvl)


def _bias_selector():
    sel = np.zeros((C_PIECES, LANES, D_MODEL), np.float32)
    for head in range(ATT_HEADS):
        pair, hh = divmod(head, 2)
        for i in range(C_PIECES):
            sel[i, head, pair * LANES + hh * C_PIECES + i] = 1.0
    return sel


def kernel(x, p, norm_pre, norm_post, att_w_in, att_b_f, att_w_out, rec_w_in, rec_lb, rec_out_norm,
           rec_w_out, ple_w_proj, ple_w_gate):
    B, S, D = x.shape
    assert D == D_MODEL and S % ROW_TILE == 0 and ROW_TILE == ATT_BLOCK and S % REC_CHUNK == 0
    assert norm_pre.shape[0] == 2 and rec_lb.shape[0] == 2
    T = B * S
    W = ATT_HEADS * ATT_HEAD_DIM

    w_in = att_w_in[0]
    wq, wk, wv, wg = (w_in[:, n * W:(n + 1) * W].astype(BF16) for n in range(4))
    wfl = jnp.pad(w_in[:, 4 * W:], ((0, 0), (0, LANES - ATT_HEADS))).astype(BF16)
    bf = jnp.pad(att_b_f[0], (0, LANES - ATT_HEADS)).reshape(1, LANES)
    tril_row = jnp.asarray(np.tril(np.ones((ROW_TILE, ROW_TILE), np.float32)), BF16)
    sel = jnp.asarray(_bias_selector(), BF16)
    hsel = jnp.asarray(np.arange(D)[:, None] // ATT_HEAD_DIM == np.arange(LANES)[None, :], BF16)
    q, k, vt, gs, caug, nblk = _fox_proj(x, norm_pre[0:1], wq, wk, wv.T, wg, wfl, bf, tril_row, sel,
                                         hsel)
    nblk = jnp.transpose(nblk[:, :, 0, :ATT_HEADS], (0, 2, 1)).reshape(-1)
    z = _fox_attn(nblk, q, k, caug, vt, gs)
    h = _post(z.reshape(T, D), x.reshape(T, D), p[0].reshape(T, PLE_DIM), att_w_out[0].astype(BF16),
              norm_post[0:1], ple_w_gate[0].astype(BF16), ple_w_proj[0].astype(BF16))

    r_in = rec_w_in[0]
    RW = REC_HEADS * REC_DIM
    rq, rf, ri, rg = (r_in[:, n * RW:(n + 1) * RW].astype(BF16) for n in range(4))
    q1, fl1, v1, gs1 = _rec_proj(h, norm_pre[1:2], rq, rf, ri, rg)
    tril_rec = jnp.asarray(np.tril(np.ones((REC_CHUNK, REC_CHUNK), np.float32)), BF16)
    lvl = jnp.asarray(_rec_level_map())
    shape3 = lambda a: a.reshape(B, S, D)
    z1 = _rec(shape3(q1), shape3(fl1), shape3(v1), shape3(gs1), rec_lb, rec_out_norm[0:1],
              tril_rec, lvl)
    out = _post(z1.reshape(T, D), h, p[1].reshape(T, PLE_DIM), rec_w_out[0].astype(BF16),
                norm_post[1:2], ple_w_gate[1].astype(BF16), ple_w_proj[1].astype(BF16))
    return out.reshape(B, S, D)
```

```python
import numpy as np
import jax
import jax.numpy as jnp
from jax import lax
from jax.experimental import pallas as pl
from jax.experimental.pallas import tpu as pltpu

F32 = jnp.float32
BF16 = jnp.bfloat16

D_MODEL = 1024
PLE_DIM = 256
ATT_HEADS = 16
ATT_HEAD_DIM = 64
REC_HEADS = 8
REC_DIM = 128
EPS = 1e-6
LOG2_E = 1.4426950408889634
SKIP_LOG2 = 130.0
NORM_MARGIN = 1.01
DOT_MARGIN = 2.0 ** -7

LANES = 128
HEAD_PAIRS = ATT_HEADS // 2
C_PIECES = 3

ROW_TILE = 512
ATT_BLOCK = 512
Q_GROUP = 4
REC_CHUNK = 256
REC_HEADS_PER_STEP = 2
REC_PIECES = 2
VMEM_LIMIT = 56 * 1024 * 1024

NT_DIMS = (((1,), (1,)), ((), ()))
TN_DIMS = (((0,), (0,)), ((), ()))


def _dot(a, b):
    return jnp.dot(a, b, preferred_element_type=F32)


def _dot_nt(a, b):
    return lax.dot_general(a, b, NT_DIMS, preferred_element_type=F32)


def _dot_tn(a, b):
    return lax.dot_general(a, b, TN_DIMS, preferred_element_type=F32)


def _split_bf16(x, pieces):
    out = []
    r = x
    for _ in range(pieces - 1):
        p = r.astype(BF16)
        out.append(p)
        r = r - p.astype(F32)
    out.append(r.astype(BF16))
    return out


def _rms_norm(x, gain):
    ms = jnp.mean(x * x, axis=-1, keepdims=True)
    return x * lax.rsqrt(ms + EPS) * gain


def _sigmoid(x):
    return 1.0 / (1.0 + jnp.exp(-x))


def _fox_proj_kernel(x_ref, gain_ref, wq_ref, wk_ref, wvt_ref, wg_ref, wfl_ref, bf_ref,
                     tril_ref, sel_ref, hsel_ref, qt_ref, k_ref, vt_ref, gs_ref, caug_ref, nblk_ref,
                     carry_ref, kmax_tab_ref, clast_tab_ref):
    tm = x_ref.shape[1]
    i = pl.program_id(1)
    u = _rms_norm(x_ref[0], gain_ref[...]).astype(BF16)
    qb = (_dot(u, wq_ref[...]) * (ATT_HEAD_DIM ** -0.5 * LOG2_E)).astype(BF16)
    kb = _dot(u, wk_ref[...]).astype(BF16)
    qt_ref[0, 0] = qb.T
    k_ref[0] = kb
    vt_ref[0, 0] = _dot_nt(wvt_ref[...], u).astype(BF16)
    g = _dot(u, wg_ref[...])
    gs_ref[0] = (g * _sigmoid(g)).astype(BF16)

    fl = _dot(u, wfl_ref[...]) + bf_ref[...]
    log_f = jnp.minimum(fl, 0.0) - jnp.log1p(jnp.exp(-jnp.abs(fl)))
    tril = tril_ref[...]
    c_blk = sum(_dot(tril, piece) for piece in _split_bf16(log_f, C_PIECES))

    @pl.when(i == 0)
    def _():
        carry_ref[...] = jnp.zeros_like(carry_ref)
        kmax_tab_ref[...] = jnp.zeros_like(kmax_tab_ref)
        clast_tab_ref[...] = jnp.zeros_like(clast_tab_ref)

    c = c_blk + carry_ref[...]
    carry_ref[...] = c[tm - 1:tm, :]
    c2 = c * LOG2_E
    caug = sum(_dot(piece, sel_ref[n]) for n, piece in enumerate(_split_bf16(-c2, C_PIECES)))
    caug_ref[0] = caug.astype(BF16)

    qf, kf = qb.astype(F32), kb.astype(F32)
    hsel = hsel_ref[...]
    per_head = lambda t: _dot(t.astype(BF16), hsel)
    qmax = jnp.sqrt(jnp.max(per_head(qf * qf), axis=0, keepdims=True)) * NORM_MARGIN
    kmax = jnp.sqrt(jnp.max(per_head(kf * kf), axis=0, keepdims=True)) * NORM_MARGIN
    m_low = jnp.min(per_head(qf * kf) - c2, axis=0, keepdims=True) - DOT_MARGIN * qmax * kmax
    kmax_tab_ref[pl.ds(i, 1), :] = kmax
    clast_tab_ref[pl.ds(i, 1), :] = c2[tm - 1:tm, :]
    blk = lax.broadcasted_iota(jnp.int32, kmax_tab_ref.shape, 0)
    bound = qmax * kmax_tab_ref[...] - clast_tab_ref[...] - m_low
    first_kept = jnp.min(jnp.where((blk < i) & (bound >= -SKIP_LOG2), blk, i), axis=0, keepdims=True)
    nblk_ref[0, 0] = jnp.broadcast_to(i - first_kept + 1, nblk_ref.shape[2:])


def _fox_proj(x, gain, wq, wk, wvt, wg, wfl, bf, tril, sel, hsel):
    B, S, D = x.shape
    tm = ROW_TILE
    nb = S // tm
    const = lambda shape: pl.BlockSpec(shape, lambda b, i: (0,) * len(shape))
    tok = pl.BlockSpec((1, tm, D), lambda b, i: (b, i, 0))
    out_tok = jax.ShapeDtypeStruct((B, S, D), BF16)
    tiled_t = pl.BlockSpec((1, 1, D, tm), lambda b, i: (b, i, 0, 0))
    out_t = jax.ShapeDtypeStruct((B, nb, D, tm), BF16)
    return pl.pallas_call(
        _fox_proj_kernel,
        grid=(B, nb),
        in_specs=[tok, const((1, D)), const((D, D)), const((D, D)), const((D, D)), const((D, D)),
                  const((D, LANES)), const((1, LANES)), const((tm, tm)), const((C_PIECES, LANES, D)),
                  const((D, LANES))],
        out_specs=[tiled_t, tok, tiled_t, tok, tok,
                   pl.BlockSpec((1, 1, 8, LANES), lambda b, i: (b, i, 0, 0))],
        out_shape=[out_t, out_tok, out_t, out_tok, out_tok,
                   jax.ShapeDtypeStruct((B, nb, 8, LANES), jnp.int32)],
        scratch_shapes=[pltpu.VMEM((1, LANES), F32), pltpu.VMEM((nb, LANES), F32),
                        pltpu.VMEM((nb, LANES), F32)],
        compiler_params=pltpu.CompilerParams(
            dimension_semantics=("arbitrary", "arbitrary"), vmem_limit_bytes=VMEM_LIMIT),
    )(x, gain, wq, wk, wvt, wg, wfl, bf, tril, sel, hsel)


def _fox_attn_kernel(nblk_ref, qt_ref, k_ref, caug_ref, vt_ref, gs_ref, z_ref,
                     s0_scr, s1_scr, s2_scr, s3_scr, mx0_scr, mx1_scr, mx2_scr, mx3_scr, m_scr, acc_scr):
    bq = bk = ATT_BLOCK
    b, pair, grp = pl.program_id(0), pl.program_id(1), pl.program_id(2)
    nb = pl.num_programs(2) * Q_GROUP
    ones_rows = jnp.ones((16, bk), BF16)
    bufs = ((s0_scr, mx0_scr), (s1_scr, mx1_scr), (s2_scr, mx2_scr), (s3_scr, mx3_scr))
    bias_row = lax.broadcasted_iota(jnp.int32, (LANES, bq), 0)
    no_head = jnp.zeros((ATT_HEAD_DIM, bq), BF16)

    def query_block(sub, carry):
        i = grp * Q_GROUP + sub
        q_t = qt_ref[0, sub]
        outs = []
        for hh in range(2):
            lo = hh * ATT_HEAD_DIM
            pick = jnp.where((bias_row >= hh * C_PIECES) & (bias_row < (hh + 1) * C_PIECES),
                             1.0, 0.0).astype(BF16)
            parts = [q_t[:ATT_HEAD_DIM], no_head] if hh == 0 else [no_head, q_t[ATT_HEAD_DIM:]]
            qt_aug = jnp.concatenate(parts + [pick], axis=0)

            def put_scores(j, dst, diagonal=False):
                rows = pl.ds(pl.multiple_of(j * bk, bk), bk)
                k_aug = jnp.concatenate([k_ref[0, rows, :], caug_ref[0, rows, :]], axis=1)
                s_t = _dot(k_aug, qt_aug)
                if diagonal:
                    key_pos = lax.broadcasted_iota(jnp.int32, (bk, bq), 0)
                    qry_pos = lax.broadcasted_iota(jnp.int32, (bk, bq), 1)
                    s_t = jnp.where(key_pos <= qry_pos, s_t, -jnp.inf)
                dst[0][...] = s_t
                dst[1][...] = jnp.max(s_t, axis=0, keepdims=True)

            def update(j, src):
                m = m_scr[...]
                m_new = jnp.maximum(m, src[1][...])
                alpha = jnp.exp2(m - m_new)
                p_t = jnp.exp2((src[0][...] - m_new).astype(BF16))
                v_aug = jnp.concatenate([vt_ref[0, j, lo:lo + ATT_HEAD_DIM, :], ones_rows], axis=0)
                acc_scr[...] = alpha * acc_scr[...] + _dot(v_aug, p_t)
                m_scr[...] = m_new

            n_steps = nblk_ref[(b * ATT_HEADS + 2 * pair + hh) * nb + i]
            m_scr[...] = jnp.full(m_scr.shape, -jnp.inf, F32)
            acc_scr[...] = jnp.zeros(acc_scr.shape, F32)
            put_scores(i, bufs[0], diagonal=True)
            put_scores(jnp.maximum(i - 1, 0), bufs[1])

            def four_steps(g, c):
                t0 = 4 * g
                for n in range(4):
                    put_scores(jnp.maximum(i - (t0 + n + 2), 0), bufs[(n + 2) % 4])
                    update(i - (t0 + n), bufs[n])
                return c

            lax.fori_loop(0, n_steps // 4, four_steps, 0)
            t0 = 4 * (n_steps // 4)
            rest = n_steps - t0

            @pl.when(rest >= 1)
            def _():
                update(i - t0, bufs[0])

            @pl.when(rest >= 2)
            def _():
                update(i - t0 - 1, bufs[1])

            @pl.when(rest == 3)
            def _():
                put_scores(i - t0 - 2, bufs[2])
                update(i - t0 - 2, bufs[2])

            acc = acc_scr[...]
            outs.append(acc[:ATT_HEAD_DIM, :] / acc[ATT_HEAD_DIM:ATT_HEAD_DIM + 1, :])

        o = jnp.concatenate(outs, axis=0).T
        rows = pl.ds(pl.multiple_of(sub * bq, bq), bq)
        z_ref[0, rows, :] = (o * gs_ref[0, rows, :].astype(F32)).astype(BF16)
        return carry

    lax.fori_loop(0, Q_GROUP, query_block, 0)


def _fox_attn(nblk, qt, k, caug, vt, gs):
    B, S, D = k.shape
    bq = ATT_BLOCK
    nb = S // bq
    tok = pl.BlockSpec((1, Q_GROUP * bq, LANES), lambda b, p, g, n: (b, g, p))
    seq = pl.BlockSpec((1, S, LANES), lambda b, p, g, n: (b, 0, p))
    return pl.pallas_call(
        _fox_attn_kernel,
        grid_spec=pltpu.PrefetchScalarGridSpec(
            num_scalar_prefetch=1,
            grid=(B, HEAD_PAIRS, nb // Q_GROUP),
            in_specs=[pl.BlockSpec((1, Q_GROUP, LANES, bq), lambda b, p, g, n: (b, g, p, 0)),
                      seq, seq,
                      pl.BlockSpec((1, nb, LANES, vt.shape[3]), lambda b, p, g, n: (b, 0, p, 0)),
                      tok],
            out_specs=tok,
            scratch_shapes=[pltpu.VMEM((ATT_BLOCK, bq), F32)] * 4 + [pltpu.VMEM((1, bq), F32)] * 4
            + [pltpu.VMEM((1, bq), F32), pltpu.VMEM((ATT_HEAD_DIM + 16, bq), F32)]),
        out_shape=jax.ShapeDtypeStruct((B, S, D), BF16),
        compiler_params=pltpu.CompilerParams(
            dimension_semantics=("arbitrary", "arbitrary", "arbitrary"), vmem_limit_bytes=VMEM_LIMIT),
    )(nblk, qt, k, caug, vt, gs)


def _post_kernel(z_ref, h_ref, p_ref, wout_ref, gain_ref, wgate_ref, wproj_ref, o_ref):
    y = _dot(z_ref[...], wout_ref[...])
    h1 = h_ref[...] + _rms_norm(y, gain_ref[...])
    gate = _sigmoid(_dot(h1.astype(BF16), wgate_ref[...]))
    o_ref[...] = h1 + _dot(p_ref[...].astype(BF16), wproj_ref[...]) * gate


def _post(z, h, p, wout, gain, wgate, wproj):
    T, D = h.shape
    tm = ROW_TILE
    const = lambda shape: pl.BlockSpec(shape, lambda i: (0,) * len(shape))
    tok = lambda w: pl.BlockSpec((tm, w), lambda i: (i, 0))
    return pl.pallas_call(
        _post_kernel,
        grid=(T // tm,),
        in_specs=[tok(D), tok(D), tok(PLE_DIM), const((D, D)), const((1, D)), const((D, D)),
                  const((PLE_DIM, D))],
        out_specs=tok(D),
        out_shape=jax.ShapeDtypeStruct((T, D), F32),
        compiler_params=pltpu.CompilerParams(
            dimension_semantics=("arbitrary",), vmem_limit_bytes=VMEM_LIMIT),
    )(z, h, p, wout, gain, wgate, wproj)


def _rec_proj_kernel(h_ref, gain_ref, wq_ref, wf_ref, wi_ref, wg_ref, q_ref, fl_ref, v_ref, gs_ref):
    u = _rms_norm(h_ref[...], gain_ref[...]).astype(BF16)
    q_ref[...] = _dot(u, wq_ref[...]).astype(BF16)
    fl_ref[...] = _dot(u, wf_ref[...])
    v_ref[...] = _dot(u, wi_ref[...]).astype(BF16)
    g = _dot(u, wg_ref[...])
    gs_ref[...] = (g * _sigmoid(g)).astype(BF16)


def _rec_proj(h, gain, wq, wf, wi, wg):
    T, D = h.shape
    tm = ROW_TILE
    const = lambda shape: pl.BlockSpec(shape, lambda i: (0,) * len(shape))
    tok = pl.BlockSpec((tm, D), lambda i: (i, 0))
    bf = jax.ShapeDtypeStruct((T, D), BF16)
    return pl.pallas_call(
        _rec_proj_kernel,
        grid=(T // tm,),
        in_specs=[tok, const((1, D)), const((D, D)), const((D, D)), const((D, D)), const((D, D))],
        out_specs=[tok, tok, tok, tok],
        out_shape=[bf, jax.ShapeDtypeStruct((T, D), F32), bf, bf],
        compiler_params=pltpu.CompilerParams(
            dimension_semantics=("arbitrary",), vmem_limit_bytes=VMEM_LIMIT),
    )(h, gain, wq, wf, wi, wg)


def _rec_levels():
    h = REC_CHUNK // 4
    out = []
    while h >= 1:
        out.append(h)
        h //= 2
    return out


def _rec_kernel(q_ref, fl_ref, v_ref, gs_ref, lb_ref, gain_ref, tril_ref, lvl_ref, z_ref, st_ref):
    C = REC_CHUNK
    HALF = C // 2

    @pl.when(pl.program_id(2) == 0)
    def _():
        st_ref[...] = jnp.zeros_like(st_ref)

    halves = (slice(0, HALF), slice(HALF, C))
    row = lax.broadcasted_iota(jnp.int32, (C, LANES), 0)
    lvl = lvl_ref[...]
    tril = tril_ref[...]

    for hd in range(REC_HEADS_PER_STEP):
        lanes = slice(hd * REC_DIM, (hd + 1) * REC_DIM)
        lbs = lb_ref[:, lanes]
        e = jnp.exp(lbs - jnp.max(lbs, axis=0, keepdims=True))
        sm = e / jnp.sum(e, axis=0, keepdims=True)
        lb = (sm[0:1] + sm[1:2]) - sm[0:1]

        fl = fl_ref[0, :, lanes]
        log_f = jnp.log(lb + (1.0 - lb) * _sigmoid(fl))
        kk = (1.0 - lb) * _sigmoid(-fl)
        qb = q_ref[0, :, lanes]
        kb = kk.astype(BF16)
        v = v_ref[0, :, lanes]
        b = sum(_dot(tril, piece) for piece in _split_bf16(log_f, REC_PIECES)) * LOG2_E

        def level_operands(ref):
            w = jnp.exp2(-jnp.abs(b - ref)).astype(BF16)
            return qb * w, kb * w

        own = lvl == 0
        a_diag = [jnp.where(own, _dot_nt(qb[r], kb[r]), 0.0) for r in halves]
        b_m1 = pltpu.roll(b, 1, 0)
        for n, h in enumerate(_rec_levels()):
            if h >= 4:
                b3 = b.reshape(C // (2 * h), 2 * h, LANES)
                ref = jnp.broadcast_to(b3[:, h - 1:h, :], b3.shape).reshape(C, LANES)
            elif h == 2:
                o4 = row & 3
                ref = jnp.where(o4 == 0, pltpu.roll(b, C - 1, 0),
                                jnp.where(o4 == 1, b, jnp.where(o4 == 2, b_m1, pltpu.roll(b, 2, 0))))
            else:
                ref = jnp.where((row & 1) == 0, b, b_m1)
            q_l, k_l = level_operands(ref)
            own = lvl == n + 1
            a_diag = [jnp.where(own, _dot_nt(q_l[r], k_l[r]), a) for r, a in zip(halves, a_diag)]
        q_l, k_l = level_operands(b[HALF - 1:HALF, :])
        a_cross = _dot_nt(q_l[halves[1]], k_l[halves[0]])
        a = jnp.concatenate([jnp.concatenate([a_diag[0], jnp.zeros_like(a_cross)], axis=1),
                             jnp.concatenate([a_cross, a_diag[1]], axis=1)], axis=0)

        st = st_ref[hd]
        q_dec = qb * jnp.exp2(b).astype(BF16)
        o = _dot(a.astype(BF16), v) + _dot_nt(q_dec, st.astype(BF16))
        b_last = b[C - 1:C, :]
        k_dec = kb * jnp.exp2(b_last - b).astype(BF16)
        st_ref[hd] = st * jnp.exp2(b_last) + _dot_tn(v, k_dec)

        o = o * lax.rsqrt(jnp.mean(o * o, axis=-1, keepdims=True) + EPS) * gain_ref[...]
        z_ref[0, :, lanes] = (o * gs_ref[0, :, lanes].astype(F32)).astype(BF16)


def _rec_level_map():
    size = REC_CHUNK // 2
    t = np.arange(size)[:, None]
    s = np.arange(size)[None, :]
    out = np.full((size, size), -1, np.int32)
    out[t == s] = 0
    for n, h in enumerate(_rec_levels()):
        own = (t // (2 * h) == s // (2 * h)) & ((t & h) != 0) & ((s & h) == 0)
        out[own] = n + 1
    return out


def _rec(q, fl, v, gs, rec_lb, gain, tril, lvl):
    B, S, D = q.shape
    C = REC_CHUNK
    width = REC_HEADS_PER_STEP * REC_DIM
    tok = pl.BlockSpec((1, C, width), lambda b, h, i: (b, i, h))
    const = lambda shape: pl.BlockSpec(shape, lambda b, h, i: (0,) * len(shape))
    return pl.pallas_call(
        _rec_kernel,
        grid=(B, REC_HEADS // REC_HEADS_PER_STEP, S // C),
        in_specs=[tok, tok, tok, tok,
                  pl.BlockSpec((rec_lb.shape[0], width), lambda b, h, i: (0, h)),
                  const((1, REC_DIM)), const((C, C)), const((C // 2, C // 2))],
        out_specs=tok,
        out_shape=jax.ShapeDtypeStruct((B, S, D), BF16),
        scratch_shapes=[pltpu.VMEM((REC_HEADS_PER_STEP, REC_DIM, REC_DIM), F32)],
        compiler_params=pltpu.CompilerParams(
            dimension_semantics=("arbitrary", "arbitrary", "arbitrary"), vmem_limit_bytes=VMEM_LIMIT),
    )(q, fl, v, gs, rec_lb, gain, tril, lvl)


def _bias_selector():
    sel = np.zeros((C_PIECES, LANES, D_MODEL), np.float32)
    for head in range(ATT_HEADS):
        pair, hh = divmod(head, 2)
        for i in range(C_PIECES):
            sel[i, head, pair * LANES + hh * C_PIECES + i] = 1.0
    return sel


def kernel(x, p, norm_pre, norm_post, att_w_in, att_b_f, att_w_out, rec_w_in, rec_lb, rec_out_norm,
           rec_w_out, ple_w_proj, ple_w_gate):
    B, S, D = x.shape
    assert D == D_MODEL and ROW_TILE == ATT_BLOCK
    assert S % (Q_GROUP * ATT_BLOCK) == 0 and S % REC_CHUNK == 0
    assert norm_pre.shape[0] == 2 and rec_lb.shape[0] == 2
    T = B * S
    W = ATT_HEADS * ATT_HEAD_DIM

    w_in = att_w_in[0]
    wq, wk, wv, wg = (w_in[:, n * W:(n + 1) * W].astype(BF16) for n in range(4))
    wfl = jnp.pad(w_in[:, 4 * W:], ((0, 0), (0, LANES - ATT_HEADS))).astype(BF16)
    bf = jnp.pad(att_b_f[0], (0, LANES - ATT_HEADS)).reshape(1, LANES)
    tril_row = jnp.asarray(np.tril(np.ones((ROW_TILE, ROW_TILE), np.float32)), BF16)
    sel = jnp.asarray(_bias_selector(), BF16)
    hsel = jnp.asarray(np.arange(D)[:, None] // ATT_HEAD_DIM == np.arange(LANES)[None, :], BF16)
    qt, k, vt, gs, caug, nblk = _fox_proj(x, norm_pre[0:1], wq, wk, wv.T, wg, wfl, bf, tril_row, sel,
                                          hsel)
    nblk = jnp.transpose(nblk[:, :, 0, :ATT_HEADS], (0, 2, 1)).reshape(-1)
    z = _fox_attn(nblk, qt, k, caug, vt, gs)
    h = _post(z.reshape(T, D), x.reshape(T, D), p[0].reshape(T, PLE_DIM), att_w_out[0].astype(BF16),
              norm_post[0:1], ple_w_gate[0].astype(BF16), ple_w_proj[0].astype(BF16))

    r_in = rec_w_in[0]
    RW = REC_HEADS * REC_DIM
    rq, rf, ri, rg = (r_in[:, n * RW:(n + 1) * RW].astype(BF16) for n in range(4))
    q1, fl1, v1, gs1 = _rec_proj(h, norm_pre[1:2], rq, rf, ri, rg)
    tril_rec = jnp.asarray(np.tril(np.ones((REC_CHUNK, REC_CHUNK), np.float32)), BF16)
    lvl = jnp.asarray(_rec_level_map())
    shape3 = lambda a: a.reshape(B, S, D)
    z1 = _rec(shape3(q1), shape3(fl1), shape3(v1), shape3(gs1), rec_lb, rec_out_norm[0:1],
              tril_rec, lvl)
    out = _post(z1.reshape(T, D), h, p[1].reshape(T, PLE_DIM), rec_w_out[0].astype(BF16),
                norm_post[1:2], ple_w_gate[1].astype(BF16), ple_w_proj[1].astype(BF16))
    return out.reshape(B, S, D)
```

```python
import numpy as np
import jax
import jax.numpy as jnp
from jax import lax
from jax.experimental import pallas as pl
from jax.experimental.pallas import tpu as pltpu

F32 = jnp.float32
BF16 = jnp.bfloat16

D_MODEL = 1024
PLE_DIM = 256
ATT_HEADS = 16
ATT_HEAD_DIM = 64
REC_HEADS = 8
REC_DIM = 128
EPS = 1e-6
LOG2_E = 1.4426950408889634
SKIP_LOG2 = 130.0
NORM_MARGIN = 1.01
DOT_MARGIN = 2.0 ** -7

LANES = 128
HEAD_PAIRS = ATT_HEADS // 2
C_PIECES = 3
BIAS_LANES_PER_HEAD = 4

ROW_TILE = 512
ATT_BLOCK = 512
Q_TILES = 2
Q_GROUP = 8
M_INIT = -1e30
REC_CHUNK = 256
REC_HEADS_PER_STEP = 2
REC_PIECES = 2
VMEM_LIMIT = 56 * 1024 * 1024

NT_DIMS = (((1,), (1,)), ((), ()))
TN_DIMS = (((0,), (0,)), ((), ()))


def _dot(a, b):
    return jnp.dot(a, b, preferred_element_type=F32)


def _dot_nt(a, b):
    return lax.dot_general(a, b, NT_DIMS, preferred_element_type=F32)


def _dot_tn(a, b):
    return lax.dot_general(a, b, TN_DIMS, preferred_element_type=F32)


def _split_bf16(x, pieces):
    out = []
    r = x
    for _ in range(pieces - 1):
        p = r.astype(BF16)
        out.append(p)
        r = r - p.astype(F32)
    out.append(r.astype(BF16))
    return out


def _rms_norm(x, gain):
    ms = jnp.mean(x * x, axis=-1, keepdims=True)
    return x * lax.rsqrt(ms + EPS) * gain


def _sigmoid(x):
    return 1.0 / (1.0 + jnp.exp(-x))


def _fox_proj_kernel(x_ref, gain_ref, wq_ref, wk_ref, wvt_ref, wg_ref, wfl_ref, bf_ref,
                     tril_ref, sel_ref, hsel_ref, qt_ref, k_ref, vt_ref, gs_ref, caug_ref, nblk_ref,
                     carry_ref, kmax_tab_ref, clast_tab_ref):
    tm = x_ref.shape[1]
    i = pl.program_id(1)
    u = _rms_norm(x_ref[0], gain_ref[...]).astype(BF16)
    qb = (_dot(u, wq_ref[...]) * (ATT_HEAD_DIM ** -0.5 * LOG2_E)).astype(BF16)
    kb = _dot(u, wk_ref[...]).astype(BF16)
    qt_ref[0, 0] = qb.T
    k_ref[0] = kb
    vt_ref[0, 0] = _dot_nt(wvt_ref[...], u).astype(BF16)
    g = _dot(u, wg_ref[...])
    gs_ref[0] = (g * _sigmoid(g)).astype(BF16)

    fl = _dot(u, wfl_ref[...]) + bf_ref[...]
    log_f = jnp.minimum(fl, 0.0) - jnp.log1p(jnp.exp(-jnp.abs(fl)))
    tril = tril_ref[...]
    c_blk = sum(_dot(tril, piece) for piece in _split_bf16(log_f, C_PIECES))

    @pl.when(i == 0)
    def _():
        carry_ref[...] = jnp.zeros_like(carry_ref)
        kmax_tab_ref[...] = jnp.zeros_like(kmax_tab_ref)
        clast_tab_ref[...] = jnp.zeros_like(clast_tab_ref)

    c = c_blk + carry_ref[...]
    carry_ref[...] = c[tm - 1:tm, :]
    c2 = c * LOG2_E
    caug = sum(_dot(piece, sel_ref[n]) for n, piece in enumerate(_split_bf16(-c2, C_PIECES)))
    caug_ref[0] = caug.astype(BF16)

    qf, kf = qb.astype(F32), kb.astype(F32)
    hsel = hsel_ref[...]
    per_head = lambda t: _dot(t.astype(BF16), hsel)
    qmax = jnp.sqrt(jnp.max(per_head(qf * qf), axis=0, keepdims=True)) * NORM_MARGIN
    kmax = jnp.sqrt(jnp.max(per_head(kf * kf), axis=0, keepdims=True)) * NORM_MARGIN
    m_low = jnp.min(per_head(qf * kf) - c2, axis=0, keepdims=True) - DOT_MARGIN * qmax * kmax
    kmax_tab_ref[pl.ds(i, 1), :] = kmax
    clast_tab_ref[pl.ds(i, 1), :] = c2[tm - 1:tm, :]
    blk = lax.broadcasted_iota(jnp.int32, kmax_tab_ref.shape, 0)
    bound = qmax * kmax_tab_ref[...] - clast_tab_ref[...] - m_low
    first_kept = jnp.min(jnp.where((blk < i) & (bound >= -SKIP_LOG2), blk, i), axis=0, keepdims=True)
    nblk_ref[0, 0] = jnp.broadcast_to(i - first_kept + 1, nblk_ref.shape[2:])


def _fox_proj(x, gain, wq, wk, wvt, wg, wfl, bf, tril, sel, hsel):
    B, S, D = x.shape
    tm = ROW_TILE
    nb = S // tm
    const = lambda shape: pl.BlockSpec(shape, lambda b, i: (0,) * len(shape))
    tok = pl.BlockSpec((1, tm, D), lambda b, i: (b, i, 0))
    out_tok = jax.ShapeDtypeStruct((B, S, D), BF16)
    tiled_t = pl.BlockSpec((1, 1, D, tm), lambda b, i: (b, i, 0, 0))
    out_t = jax.ShapeDtypeStruct((B, nb, D, tm), BF16)
    return pl.pallas_call(
        _fox_proj_kernel,
        grid=(B, nb),
        in_specs=[tok, const((1, D)), const((D, D)), const((D, D)), const((D, D)), const((D, D)),
                  const((D, LANES)), const((1, LANES)), const((tm, tm)), const((C_PIECES, LANES, LANES)),
                  const((D, LANES))],
        out_specs=[tiled_t, tok, tiled_t, tok, pl.BlockSpec((1, tm, LANES), lambda b, i: (b, i, 0)),
                   pl.BlockSpec((1, 1, 8, LANES), lambda b, i: (b, i, 0, 0))],
        out_shape=[out_t, out_tok, out_t, out_tok, jax.ShapeDtypeStruct((B, S, LANES), BF16),
                   jax.ShapeDtypeStruct((B, nb, 8, LANES), jnp.int32)],
        scratch_shapes=[pltpu.VMEM((1, LANES), F32), pltpu.VMEM((nb, LANES), F32),
                        pltpu.VMEM((nb, LANES), F32)],
        compiler_params=pltpu.CompilerParams(
            dimension_semantics=("arbitrary", "arbitrary"), vmem_limit_bytes=VMEM_LIMIT),
    )(x, gain, wq, wk, wvt, wg, wfl, bf, tril, sel, hsel)


def _fox_attn_kernel(nblk_ref, qt_ref, k_ref, caug_ref, vt_ref, gs_ref, z_ref,
                     s0_scr, s1_scr, s2_scr, s3_scr, mx0_scr, mx1_scr, mx2_scr, mx3_scr, m_scr, acc_scr):
    bk = ATT_BLOCK
    bq = Q_TILES * ATT_BLOCK
    b, pair, grp = pl.program_id(0), pl.program_id(1), pl.program_id(2)
    nb = pl.num_programs(2) * Q_GROUP
    ones_rows = jnp.ones((16, bk), BF16)
    bufs = ((s0_scr, mx0_scr), (s1_scr, mx1_scr), (s2_scr, mx2_scr), (s3_scr, mx3_scr))
    bias_row = lax.broadcasted_iota(jnp.int32, (LANES, bq), 0)
    no_head = jnp.zeros((ATT_HEAD_DIM, bq), BF16)

    def query_unit(sub, carry):
        unit = grp * (Q_GROUP // Q_TILES) + sub
        last = Q_TILES * unit + Q_TILES - 1
        q_t = jnp.concatenate([qt_ref[0, Q_TILES * sub + n] for n in range(Q_TILES)], axis=1)
        outs = []
        for hh in range(2):
            lo = hh * ATT_HEAD_DIM
            first_lane = BIAS_LANES_PER_HEAD * (2 * pair + hh)
            pick = jnp.where((bias_row >= first_lane) & (bias_row < first_lane + C_PIECES),
                             1.0, 0.0).astype(BF16)
            parts = [q_t[:ATT_HEAD_DIM], no_head] if hh == 0 else [no_head, q_t[ATT_HEAD_DIM:]]
            qt_aug = jnp.concatenate(parts + [pick], axis=0)

            def put_scores(j, dst, diagonal=False):
                rows = pl.ds(pl.multiple_of(j * bk, bk), bk)
                k_aug = jnp.concatenate([k_ref[0, rows, :], caug_ref[0, rows, :]], axis=1)
                s_t = _dot(k_aug, qt_aug)
                if diagonal:
                    key_pos = j * bk + lax.broadcasted_iota(jnp.int32, (bk, bq), 0)
                    qry_pos = unit * bq + lax.broadcasted_iota(jnp.int32, (bk, bq), 1)
                    s_t = jnp.where(key_pos <= qry_pos, s_t, -jnp.inf)
                dst[0][...] = s_t
                dst[1][...] = jnp.max(s_t, axis=0, keepdims=True)

            def update(j, src):
                m = m_scr[...]
                m_new = jnp.maximum(m, src[1][...])
                alpha = jnp.exp2(m - m_new)
                p_t = jnp.exp2((src[0][...] - m_new).astype(BF16))
                v_aug = jnp.concatenate([vt_ref[0, j, lo:lo + ATT_HEAD_DIM, :], ones_rows], axis=0)
                acc_scr[...] = alpha * acc_scr[...] + _dot(v_aug, p_t)
                m_scr[...] = m_new

            head_row = (b * ATT_HEADS + 2 * pair + hh) * nb
            n_steps = nblk_ref[head_row + last]
            for n in range(1, Q_TILES):
                n_steps = jnp.maximum(n_steps, nblk_ref[head_row + last - n] + n)
            m_scr[...] = jnp.full(m_scr.shape, M_INIT, F32)
            acc_scr[...] = jnp.zeros(acc_scr.shape, F32)
            put_scores(last, bufs[0], diagonal=True)
            put_scores(last - 1, bufs[1], diagonal=True)

            def four_steps(g, c):
                t0 = 4 * g
                for n in range(4):
                    put_scores(jnp.maximum(last - (t0 + n + 2), 0), bufs[(n + 2) % 4])
                    update(last - (t0 + n), bufs[n])
                return c

            lax.fori_loop(0, n_steps // 4, four_steps, 0)
            t0 = 4 * (n_steps // 4)
            rest = n_steps - t0

            @pl.when(rest == 1)
            def _():
                update(last - t0, bufs[0])

            @pl.when(rest == 2)
            def _():
                update(last - t0, bufs[0])
                update(last - t0 - 1, bufs[1])

            @pl.when(rest == 3)
            def _():
                put_scores(last - t0 - 2, bufs[2])
                update(last - t0, bufs[0])
                update(last - t0 - 1, bufs[1])
                update(last - t0 - 2, bufs[2])

            acc = acc_scr[...]
            outs.append(acc[:ATT_HEAD_DIM, :] / acc[ATT_HEAD_DIM:ATT_HEAD_DIM + 1, :])

        o = jnp.concatenate(outs, axis=0).T
        rows = pl.ds(pl.multiple_of(sub * bq, bq), bq)
        z_ref[0, rows, :] = (o * gs_ref[0, rows, :].astype(F32)).astype(BF16)
        return carry

    lax.fori_loop(0, Q_GROUP // Q_TILES, query_unit, 0)


def _fox_attn(nblk, qt, k, caug, vt, gs):
    B, S, D = k.shape
    bk = ATT_BLOCK
    bq = Q_TILES * ATT_BLOCK
    nb = S // bk
    tok = pl.BlockSpec((1, Q_GROUP * bk, LANES), lambda b, p, g, n: (b, g, p))
    seq = pl.BlockSpec((1, S, LANES), lambda b, p, g, n: (b, 0, p))
    return pl.pallas_call(
        _fox_attn_kernel,
        grid_spec=pltpu.PrefetchScalarGridSpec(
            num_scalar_prefetch=1,
            grid=(B, HEAD_PAIRS, nb // Q_GROUP),
            in_specs=[pl.BlockSpec((1, Q_GROUP, LANES, bk), lambda b, p, g, n: (b, g, p, 0)),
                      seq, pl.BlockSpec((1, S, LANES), lambda b, p, g, n: (b, 0, 0)),
                      pl.BlockSpec((1, nb, LANES, vt.shape[3]), lambda b, p, g, n: (b, 0, p, 0)),
                      tok],
            out_specs=tok,
            scratch_shapes=[pltpu.VMEM((bk, bq), F32)] * 4 + [pltpu.VMEM((1, bq), F32)] * 4
            + [pltpu.VMEM((1, bq), F32), pltpu.VMEM((ATT_HEAD_DIM + 16, bq), F32)]),
        out_shape=jax.ShapeDtypeStruct((B, S, D), BF16),
        compiler_params=pltpu.CompilerParams(
            dimension_semantics=("arbitrary", "arbitrary", "arbitrary"), vmem_limit_bytes=VMEM_LIMIT),
    )(nblk, qt, k, caug, vt, gs)


def _post_block(z_ref, h_ref, p_ref, wout_ref, gain_ref, wgate_ref, wproj_ref):
    y = _dot(z_ref[...], wout_ref[...])
    h1 = h_ref[...] + _rms_norm(y, gain_ref[...])
    gate = _sigmoid(_dot(h1.astype(BF16), wgate_ref[...]))
    return h1 + _dot(p_ref[...].astype(BF16), wproj_ref[...]) * gate


def _post_kernel(z_ref, h_ref, p_ref, wout_ref, gain_ref, wgate_ref, wproj_ref, o_ref):
    o_ref[...] = _post_block(z_ref, h_ref, p_ref, wout_ref, gain_ref, wgate_ref, wproj_ref)


def _post(z, h, p, wout, gain, wgate, wproj):
    T, D = h.shape
    tm = ROW_TILE
    const = lambda shape: pl.BlockSpec(shape, lambda i: (0,) * len(shape))
    tok = lambda w: pl.BlockSpec((tm, w), lambda i: (i, 0))
    return pl.pallas_call(
        _post_kernel,
        grid=(T // tm,),
        in_specs=[tok(D), tok(D), tok(PLE_DIM), const((D, D)), const((1, D)), const((D, D)),
                  const((PLE_DIM, D))],
        out_specs=tok(D),
        out_shape=jax.ShapeDtypeStruct((T, D), F32),
        compiler_params=pltpu.CompilerParams(
            dimension_semantics=("arbitrary",), vmem_limit_bytes=VMEM_LIMIT),
    )(z, h, p, wout, gain, wgate, wproj)


def _post_rec_proj_kernel(z_ref, h_ref, p_ref, wout_ref, gain_ref, wgate_ref, wproj_ref,
                          gain1_ref, wq_ref, wf_ref, wi_ref, wg_ref,
                          o_ref, q_ref, fl_ref, v_ref, gs_ref):
    h_new = _post_block(z_ref, h_ref, p_ref, wout_ref, gain_ref, wgate_ref, wproj_ref)
    o_ref[...] = h_new
    u = _rms_norm(h_new, gain1_ref[...]).astype(BF16)
    q_ref[...] = _dot(u, wq_ref[...]).astype(BF16)
    fl_ref[...] = _dot(u, wf_ref[...])
    v_ref[...] = _dot(u, wi_ref[...]).astype(BF16)
    g = _dot(u, wg_ref[...])
    gs_ref[...] = (g * _sigmoid(g)).astype(BF16)


def _post_rec_proj(z, h, p, wout, gain, wgate, wproj, gain1, wq, wf, wi, wg):
    T, D = h.shape
    tm = ROW_TILE
    const = lambda shape: pl.BlockSpec(shape, lambda i: (0,) * len(shape))
    tok = lambda w: pl.BlockSpec((tm, w), lambda i: (i, 0))
    bf = jax.ShapeDtypeStruct((T, D), BF16)
    f32 = jax.ShapeDtypeStruct((T, D), F32)
    return pl.pallas_call(
        _post_rec_proj_kernel,
        grid=(T // tm,),
        in_specs=[tok(D), tok(D), tok(PLE_DIM), const((D, D)), const((1, D)), const((D, D)),
                  const((PLE_DIM, D)), const((1, D)), const((D, D)), const((D, D)), const((D, D)),
                  const((D, D))],
        out_specs=[tok(D)] * 5,
        out_shape=[f32, bf, f32, bf, bf],
        compiler_params=pltpu.CompilerParams(
            dimension_semantics=("arbitrary",), vmem_limit_bytes=VMEM_LIMIT),
    )(z, h, p, wout, gain, wgate, wproj, gain1, wq, wf, wi, wg)


def _rec_levels():
    h = REC_CHUNK // 4
    out = []
    while h >= 1:
        out.append(h)
        h //= 2
    return out


def _rec_kernel(q_ref, fl_ref, v_ref, gs_ref, lb_ref, gain_ref, tril_ref, lvl_ref, z_ref, st_ref):
    C = REC_CHUNK
    HALF = C // 2

    @pl.when(pl.program_id(2) == 0)
    def _():
        st_ref[...] = jnp.zeros_like(st_ref)

    halves = (slice(0, HALF), slice(HALF, C))
    row = lax.broadcasted_iota(jnp.int32, (C, LANES), 0)
    lvl = lvl_ref[...]
    tril = tril_ref[...]

    for hd in range(REC_HEADS_PER_STEP):
        lanes = slice(hd * REC_DIM, (hd + 1) * REC_DIM)
        lbs = lb_ref[:, lanes]
        e = jnp.exp(lbs - jnp.max(lbs, axis=0, keepdims=True))
        sm = e / jnp.sum(e, axis=0, keepdims=True)
        lb = (sm[0:1] + sm[1:2]) - sm[0:1]

        fl = fl_ref[0, :, lanes]
        log_f = jnp.log(lb + (1.0 - lb) * _sigmoid(fl))
        kk = (1.0 - lb) * _sigmoid(-fl)
        qb = q_ref[0, :, lanes]
        kb = kk.astype(BF16)
        v = v_ref[0, :, lanes]
        b = sum(_dot(tril, piece) for piece in _split_bf16(log_f, REC_PIECES)) * LOG2_E

        def level_operands(ref):
            w = jnp.exp2(-jnp.abs(b - ref)).astype(BF16)
            return qb * w, kb * w

        own = lvl == 0
        a_diag = [jnp.where(own, _dot_nt(qb[r], kb[r]), 0.0) for r in halves]
        b_m1 = pltpu.roll(b, 1, 0)
        for n, h in enumerate(_rec_levels()):
            if h >= 4:
                b3 = b.reshape(C // (2 * h), 2 * h, LANES)
                ref = jnp.broadcast_to(b3[:, h - 1:h, :], b3.shape).reshape(C, LANES)
            elif h == 2:
                o4 = row & 3
                ref = jnp.where(o4 == 0, pltpu.roll(b, C - 1, 0),
                                jnp.where(o4 == 1, b, jnp.where(o4 == 2, b_m1, pltpu.roll(b, 2, 0))))
            else:
                ref = jnp.where((row & 1) == 0, b, b_m1)
            q_l, k_l = level_operands(ref)
            own = lvl == n + 1
            a_diag = [jnp.where(own, _dot_nt(q_l[r], k_l[r]), a) for r, a in zip(halves, a_diag)]
        q_l, k_l = level_operands(b[HALF - 1:HALF, :])
        a_cross = _dot_nt(q_l[halves[1]], k_l[halves[0]])
        a = jnp.concatenate([jnp.concatenate([a_diag[0], jnp.zeros_like(a_cross)], axis=1),
                             jnp.concatenate([a_cross, a_diag[1]], axis=1)], axis=0)

        st = st_ref[hd]
        q_dec = qb * jnp.exp2(b).astype(BF16)
        o = _dot(a.astype(BF16), v) + _dot_nt(q_dec, st.astype(BF16))
        b_last = b[C - 1:C, :]
        k_dec = kb * jnp.exp2(b_last - b).astype(BF16)
        st_ref[hd] = st * jnp.exp2(b_last) + _dot_tn(v, k_dec)

        o = o * lax.rsqrt(jnp.mean(o * o, axis=-1, keepdims=True) + EPS) * gain_ref[...]
        z_ref[0, :, lanes] = (o * gs_ref[0, :, lanes].astype(F32)).astype(BF16)


def _rec_level_map():
    size = REC_CHUNK // 2
    t = np.arange(size)[:, None]
    s = np.arange(size)[None, :]
    out = np.full((size, size), -1, np.int32)
    out[t == s] = 0
    for n, h in enumerate(_rec_levels()):
        own = (t // (2 * h) == s // (2 * h)) & ((t & h) != 0) & ((s & h) == 0)
        out[own] = n + 1
    return out


def _rec(q, fl, v, gs, rec_lb, gain, tril, lvl):
    B, S, D = q.shape
    C = REC_CHUNK
    width = REC_HEADS_PER_STEP * REC_DIM
    tok = pl.BlockSpec((1, C, width), lambda b, h, i: (b, i, h))
    const = lambda shape: pl.BlockSpec(shape, lambda b, h, i: (0,) * len(shape))
    return pl.pallas_call(
        _rec_kernel,
        grid=(B, REC_HEADS // REC_HEADS_PER_STEP, S // C),
        in_specs=[tok, tok, tok, tok,
                  pl.BlockSpec((rec_lb.shape[0], width), lambda b, h, i: (0, h)),
                  const((1, REC_DIM)), const((C, C)), const((C // 2, C // 2))],
        out_specs=tok,
        out_shape=jax.ShapeDtypeStruct((B, S, D), BF16),
        scratch_shapes=[pltpu.VMEM((REC_HEADS_PER_STEP, REC_DIM, REC_DIM), F32)],
        compiler_params=pltpu.CompilerParams(
            dimension_semantics=("arbitrary", "arbitrary", "arbitrary"), vmem_limit_bytes=VMEM_LIMIT),
    )(q, fl, v, gs, rec_lb, gain, tril, lvl)


def _bias_lane(head, piece):
    return BIAS_LANES_PER_HEAD * head + piece


def _bias_selector():
    sel = np.zeros((C_PIECES, LANES, LANES), np.float32)
    for head in range(ATT_HEADS):
        for i in range(C_PIECES):
            sel[i, head, _bias_lane(head, i)] = 1.0
    return sel


def kernel(x, p, norm_pre, norm_post, att_w_in, att_b_f, att_w_out, rec_w_in, rec_lb, rec_out_norm,
           rec_w_out, ple_w_proj, ple_w_gate):
    B, S, D = x.shape
    assert D == D_MODEL and ROW_TILE == ATT_BLOCK
    assert S % (Q_GROUP * ATT_BLOCK) == 0 and S % REC_CHUNK == 0
    assert norm_pre.shape[0] == 2 and rec_lb.shape[0] == 2
    T = B * S
    W = ATT_HEADS * ATT_HEAD_DIM

    w_in = att_w_in[0]
    wq, wk, wv, wg = (w_in[:, n * W:(n + 1) * W].astype(BF16) for n in range(4))
    wfl = jnp.pad(w_in[:, 4 * W:], ((0, 0), (0, LANES - ATT_HEADS))).astype(BF16)
    bf = jnp.pad(att_b_f[0], (0, LANES - ATT_HEADS)).reshape(1, LANES)
    tril_row = jnp.asarray(np.tril(np.ones((ROW_TILE, ROW_TILE), np.float32)), BF16)
    sel = jnp.asarray(_bias_selector(), BF16)
    hsel = jnp.asarray(np.arange(D)[:, None] // ATT_HEAD_DIM == np.arange(LANES)[None, :], BF16)
    qt, k, vt, gs, caug, nblk = _fox_proj(x, norm_pre[0:1], wq, wk, wv.T, wg, wfl, bf, tril_row, sel,
                                          hsel)
    nblk = jnp.transpose(nblk[:, :, 0, :ATT_HEADS], (0, 2, 1)).reshape(-1)
    z = _fox_attn(nblk, qt, k, caug, vt, gs)
    r_in = rec_w_in[0]
    RW = REC_HEADS * REC_DIM
    rq, rf, ri, rg = (r_in[:, n * RW:(n + 1) * RW].astype(BF16) for n in range(4))
    h, q1, fl1, v1, gs1 = _post_rec_proj(
        z.reshape(T, D), x.reshape(T, D), p[0].reshape(T, PLE_DIM), att_w_out[0].astype(BF16),
        norm_post[0:1], ple_w_gate[0].astype(BF16), ple_w_proj[0].astype(BF16),
        norm_pre[1:2], rq, rf, ri, rg)
    tril_rec = jnp.asarray(np.tril(np.ones((REC_CHUNK, REC_CHUNK), np.float32)), BF16)
    lvl = jnp.asarray(_rec_level_map())
    shape3 = lambda a: a.reshape(B, S, D)
    z1 = _rec(shape3(q1), shape3(fl1), shape3(v1), shape3(gs1), rec_lb, rec_out_norm[0:1],
              tril_rec, lvl)
    out = _post(z1.reshape(T, D), h, p[1].reshape(T, PLE_DIM), rec_w_out[0].astype(BF16),
                norm_post[1:2], ple_w_gate[1].astype(BF16), ple_w_proj[1].astype(BF16))
    return out.reshape(B, S, D)
```

```python
import numpy as np
import jax
import jax.numpy as jnp
from jax import lax
from jax.experimental import pallas as pl
from jax.experimental.pallas import tpu as pltpu

F32 = jnp.float32
BF16 = jnp.bfloat16

D_MODEL = 1024
PLE_DIM = 256
ATT_HEADS = 16
ATT_HEAD_DIM = 64
REC_HEADS = 8
REC_DIM = 128
EPS = 1e-6
LOG2_E = 1.4426950408889634
SKIP_LOG2 = 130.0
NORM_MARGIN = 1.01
DOT_MARGIN = 2.0 ** -7

LANES = 128
HEAD_PAIRS = ATT_HEADS // 2
C_PIECES = 3
BIAS_LANES_PER_HEAD = 4

ROW_TILE = 512
ATT_BLOCK = 512
Q_TILES = 2
Q_GROUP = 8
M_INIT = -1e30
REC_CHUNK = 256
REC_HEADS_PER_STEP = 8
REC_PIECES = 2
REC_QUARTER = 64
REC_MILD_DECAY = 100.0
VMEM_LIMIT = 56 * 1024 * 1024

NT_DIMS = (((1,), (1,)), ((), ()))
TN_DIMS = (((0,), (0,)), ((), ()))


def _dot(a, b):
    return jnp.dot(a, b, preferred_element_type=F32)


def _dot_nt(a, b):
    return lax.dot_general(a, b, NT_DIMS, preferred_element_type=F32)


def _dot_tn(a, b):
    return lax.dot_general(a, b, TN_DIMS, preferred_element_type=F32)


def _split_bf16(x, pieces):
    out = []
    r = x
    for _ in range(pieces - 1):
        p = r.astype(BF16)
        out.append(p)
        r = r - p.astype(F32)
    out.append(r.astype(BF16))
    return out


def _rms_norm(x, gain):
    ms = jnp.mean(x * x, axis=-1, keepdims=True)
    return x * lax.rsqrt(ms + EPS) * gain


def _sigmoid(x):
    return 1.0 / (1.0 + jnp.exp(-x))


def _fox_proj_kernel(x_ref, gain_ref, wq_ref, wk_ref, wvt_ref, wg_ref, wfl_ref, bf_ref,
                     tril_ref, sel_ref, hsel_ref, qt_ref, k_ref, vt_ref, gs_ref, caug_ref, nblk_ref,
                     carry_ref, kmax_tab_ref, clast_tab_ref):
    tm = x_ref.shape[1]
    i = pl.program_id(1)
    u = _rms_norm(x_ref[0], gain_ref[...]).astype(BF16)
    qb = (_dot(u, wq_ref[...]) * (ATT_HEAD_DIM ** -0.5 * LOG2_E)).astype(BF16)
    kb = _dot(u, wk_ref[...]).astype(BF16)
    qt_ref[0, 0] = qb.T
    k_ref[0] = kb
    vt_ref[0, 0] = _dot_nt(wvt_ref[...], u).astype(BF16)
    g = _dot(u, wg_ref[...])
    gs_ref[0] = (g * _sigmoid(g)).astype(BF16)

    fl = _dot(u, wfl_ref[...]) + bf_ref[...]
    log_f = jnp.minimum(fl, 0.0) - jnp.log1p(jnp.exp(-jnp.abs(fl)))
    tril = tril_ref[...]
    c_blk = sum(_dot(tril, piece) for piece in _split_bf16(log_f, C_PIECES))

    @pl.when(i == 0)
    def _():
        carry_ref[...] = jnp.zeros_like(carry_ref)
        kmax_tab_ref[...] = jnp.zeros_like(kmax_tab_ref)
        clast_tab_ref[...] = jnp.zeros_like(clast_tab_ref)

    c = c_blk + carry_ref[...]
    carry_ref[...] = c[tm - 1:tm, :]
    c2 = c * LOG2_E
    caug = sum(_dot(piece, sel_ref[n]) for n, piece in enumerate(_split_bf16(-c2, C_PIECES)))
    caug_ref[0] = caug.astype(BF16)

    qf, kf = qb.astype(F32), kb.astype(F32)
    hsel = hsel_ref[...]
    per_head = lambda t: _dot(t.astype(BF16), hsel)
    qmax = jnp.sqrt(jnp.max(per_head(qf * qf), axis=0, keepdims=True)) * NORM_MARGIN
    kmax = jnp.sqrt(jnp.max(per_head(kf * kf), axis=0, keepdims=True)) * NORM_MARGIN
    m_low = jnp.min(per_head(qf * kf) - c2, axis=0, keepdims=True) - DOT_MARGIN * qmax * kmax
    kmax_tab_ref[pl.ds(i, 1), :] = kmax
    clast_tab_ref[pl.ds(i, 1), :] = c2[tm - 1:tm, :]
    blk = lax.broadcasted_iota(jnp.int32, kmax_tab_ref.shape, 0)
    bound = qmax * kmax_tab_ref[...] - clast_tab_ref[...] - m_low
    first_kept = jnp.min(jnp.where((blk < i) & (bound >= -SKIP_LOG2), blk, i), axis=0, keepdims=True)
    nblk_ref[0, 0] = jnp.broadcast_to(i - first_kept + 1, nblk_ref.shape[2:])


def _fox_proj(x, gain, wq, wk, wvt, wg, wfl, bf, tril, sel, hsel):
    B, S, D = x.shape
    tm = ROW_TILE
    nb = S // tm
    const = lambda shape: pl.BlockSpec(shape, lambda b, i: (0,) * len(shape))
    tok = pl.BlockSpec((1, tm, D), lambda b, i: (b, i, 0))
    out_tok = jax.ShapeDtypeStruct((B, S, D), BF16)
    tiled_t = pl.BlockSpec((1, 1, D, tm), lambda b, i: (b, i, 0, 0))
    out_t = jax.ShapeDtypeStruct((B, nb, D, tm), BF16)
    return pl.pallas_call(
        _fox_proj_kernel,
        grid=(B, nb),
        in_specs=[tok, const((1, D)), const((D, D)), const((D, D)), const((D, D)), const((D, D)),
                  const((D, LANES)), const((1, LANES)), const((tm, tm)), const((C_PIECES, LANES, LANES)),
                  const((D, LANES))],
        out_specs=[tiled_t, tok, tiled_t, tok, pl.BlockSpec((1, tm, LANES), lambda b, i: (b, i, 0)),
                   pl.BlockSpec((1, 1, 8, LANES), lambda b, i: (b, i, 0, 0))],
        out_shape=[out_t, out_tok, out_t, out_tok, jax.ShapeDtypeStruct((B, S, LANES), BF16),
                   jax.ShapeDtypeStruct((B, nb, 8, LANES), jnp.int32)],
        scratch_shapes=[pltpu.VMEM((1, LANES), F32), pltpu.VMEM((nb, LANES), F32),
                        pltpu.VMEM((nb, LANES), F32)],
        compiler_params=pltpu.CompilerParams(
            dimension_semantics=("arbitrary", "arbitrary"), vmem_limit_bytes=VMEM_LIMIT),
    )(x, gain, wq, wk, wvt, wg, wfl, bf, tril, sel, hsel)


def _fox_attn_kernel(nblk_ref, qt_ref, k_ref, caug_ref, vt_ref, gs_ref, z_ref,
                     s0_scr, s1_scr, s2_scr, s3_scr, mx0_scr, mx1_scr, mx2_scr, mx3_scr, m_scr, acc_scr):
    bk = ATT_BLOCK
    bq = Q_TILES * ATT_BLOCK
    b, pair, grp = pl.program_id(0), pl.program_id(1), pl.program_id(2)
    nb = pl.num_programs(2) * Q_GROUP
    ones_rows = jnp.ones((16, bk), BF16)
    bufs = ((s0_scr, mx0_scr), (s1_scr, mx1_scr), (s2_scr, mx2_scr), (s3_scr, mx3_scr))
    bias_row = lax.broadcasted_iota(jnp.int32, (LANES, bq), 0)
    no_head = jnp.zeros((ATT_HEAD_DIM, bq), BF16)

    def query_unit(sub, carry):
        unit = grp * (Q_GROUP // Q_TILES) + sub
        last = Q_TILES * unit + Q_TILES - 1
        q_t = jnp.concatenate([qt_ref[0, Q_TILES * sub + n] for n in range(Q_TILES)], axis=1)
        outs = []
        for hh in range(2):
            lo = hh * ATT_HEAD_DIM
            first_lane = BIAS_LANES_PER_HEAD * (2 * pair + hh)
            pick = jnp.where((bias_row >= first_lane) & (bias_row < first_lane + C_PIECES),
                             1.0, 0.0).astype(BF16)
            parts = [q_t[:ATT_HEAD_DIM], no_head] if hh == 0 else [no_head, q_t[ATT_HEAD_DIM:]]
            qt_aug = jnp.concatenate(parts + [pick], axis=0)

            def put_scores(j, dst, cols=slice(0, bq), diagonal=False):
                rows = pl.ds(pl.multiple_of(j * bk, bk), bk)
                k_aug = jnp.concatenate([k_ref[0, rows, :], caug_ref[0, rows, :]], axis=1)
                s_t = _dot(k_aug, qt_aug[:, cols])
                if diagonal:
                    shape = (bk, cols.stop - cols.start)
                    key_pos = j * bk + lax.broadcasted_iota(jnp.int32, shape, 0)
                    qry_pos = unit * bq + cols.start + lax.broadcasted_iota(jnp.int32, shape, 1)
                    s_t = jnp.where(key_pos <= qry_pos, s_t, -jnp.inf)
                dst[0][:, cols] = s_t
                dst[1][:, cols] = jnp.max(s_t, axis=0, keepdims=True)

            def update(j, src, cols=slice(0, bq)):
                m = m_scr[:, cols]
                m_new = jnp.maximum(m, src[1][:, cols])
                alpha = jnp.exp2(m - m_new)
                p_t = jnp.exp2((src[0][:, cols] - m_new).astype(BF16))
                v_aug = jnp.concatenate([vt_ref[0, j, lo:lo + ATT_HEAD_DIM, :], ones_rows], axis=0)
                acc_scr[:, cols] = alpha * acc_scr[:, cols] + _dot(v_aug, p_t)
                m_scr[:, cols] = m_new

            head_row = (b * ATT_HEADS + 2 * pair + hh) * nb
            n_steps = nblk_ref[head_row + last]
            for n in range(1, Q_TILES):
                n_steps = jnp.maximum(n_steps, nblk_ref[head_row + last - n] + n)
            m_scr[...] = jnp.full(m_scr.shape, M_INIT, F32)
            acc_scr[...] = jnp.zeros(acc_scr.shape, F32)
            last_tile = slice(bq - bk, bq)
            put_scores(last, bufs[0], last_tile, diagonal=True)
            put_scores(last - 1, bufs[1], diagonal=True)
            put_scores(jnp.maximum(last - 2, 0), bufs[2])
            update(last, bufs[0], last_tile)

            def four_steps(g, c):
                t0 = 1 + 4 * g
                for n in range(4):
                    put_scores(jnp.maximum(last - (t0 + n + 2), 0), bufs[(n + 3) % 4])
                    update(last - (t0 + n), bufs[(n + 1) % 4])
                return c

            groups = (n_steps - 1) // 4
            lax.fori_loop(0, groups, four_steps, 0)
            t0 = 1 + 4 * groups
            rest = n_steps - t0

            @pl.when(rest == 1)
            def _():
                update(last - t0, bufs[1])

            @pl.when(rest == 2)
            def _():
                update(last - t0, bufs[1])
                update(last - t0 - 1, bufs[2])

            @pl.when(rest == 3)
            def _():
                put_scores(last - t0 - 2, bufs[3])
                update(last - t0, bufs[1])
                update(last - t0 - 1, bufs[2])
                update(last - t0 - 2, bufs[3])

            acc = acc_scr[...]
            outs.append(acc[:ATT_HEAD_DIM, :] / acc[ATT_HEAD_DIM:ATT_HEAD_DIM + 1, :])

        o = jnp.concatenate(outs, axis=0).T
        rows = pl.ds(pl.multiple_of(sub * bq, bq), bq)
        z_ref[0, rows, :] = (o * gs_ref[0, rows, :].astype(F32)).astype(BF16)
        return carry

    lax.fori_loop(0, Q_GROUP // Q_TILES, query_unit, 0)


def _fox_attn(nblk, qt, k, caug, vt, gs):
    B, S, D = k.shape
    bk = ATT_BLOCK
    bq = Q_TILES * ATT_BLOCK
    nb = S // bk
    tok = pl.BlockSpec((1, Q_GROUP * bk, LANES), lambda b, p, g, n: (b, g, p))
    seq = pl.BlockSpec((1, S, LANES), lambda b, p, g, n: (b, 0, p))
    return pl.pallas_call(
        _fox_attn_kernel,
        grid_spec=pltpu.PrefetchScalarGridSpec(
            num_scalar_prefetch=1,
            grid=(B, HEAD_PAIRS, nb // Q_GROUP),
            in_specs=[pl.BlockSpec((1, Q_GROUP, LANES, bk), lambda b, p, g, n: (b, g, p, 0)),
                      seq, pl.BlockSpec((1, S, LANES), lambda b, p, g, n: (b, 0, 0)),
                      pl.BlockSpec((1, nb, LANES, vt.shape[3]), lambda b, p, g, n: (b, 0, p, 0)),
                      tok],
            out_specs=tok,
            scratch_shapes=[pltpu.VMEM((bk, bq), F32)] * 4 + [pltpu.VMEM((1, bq), F32)] * 4
            + [pltpu.VMEM((1, bq), F32), pltpu.VMEM((ATT_HEAD_DIM + 16, bq), F32)]),
        out_shape=jax.ShapeDtypeStruct((B, S, D), BF16),
        compiler_params=pltpu.CompilerParams(
            dimension_semantics=("arbitrary", "arbitrary", "arbitrary"), vmem_limit_bytes=VMEM_LIMIT),
    )(nblk, qt, k, caug, vt, gs)


def _post_block(z_ref, h_ref, p_ref, wout_ref, gain_ref, wgate_ref, wproj_ref):
    y = _dot(z_ref[...], wout_ref[...])
    h1 = h_ref[...] + _rms_norm(y, gain_ref[...])
    gate = _sigmoid(_dot(h1.astype(BF16), wgate_ref[...]))
    return h1 + _dot(p_ref[...].astype(BF16), wproj_ref[...]) * gate


def _post_kernel(z_ref, h_ref, p_ref, wout_ref, gain_ref, wgate_ref, wproj_ref, o_ref):
    o_ref[...] = _post_block(z_ref, h_ref, p_ref, wout_ref, gain_ref, wgate_ref, wproj_ref)


def _post(z, h, p, wout, gain, wgate, wproj):
    T, D = h.shape
    tm = ROW_TILE
    const = lambda shape: pl.BlockSpec(shape, lambda i: (0,) * len(shape))
    tok = lambda w: pl.BlockSpec((tm, w), lambda i: (i, 0))
    return pl.pallas_call(
        _post_kernel,
        grid=(T // tm,),
        in_specs=[tok(D), tok(D), tok(PLE_DIM), const((D, D)), const((1, D)), const((D, D)),
                  const((PLE_DIM, D))],
        out_specs=tok(D),
        out_shape=jax.ShapeDtypeStruct((T, D), F32),
        compiler_params=pltpu.CompilerParams(
            dimension_semantics=("arbitrary",), vmem_limit_bytes=VMEM_LIMIT),
    )(z, h, p, wout, gain, wgate, wproj)


def _post_rec_proj_kernel(z_ref, h_ref, p_ref, wout_ref, gain_ref, wgate_ref, wproj_ref,
                          gain1_ref, wq_ref, wf_ref, wi_ref, wg_ref,
                          o_ref, q_ref, fl_ref, v_ref, gs_ref):
    h_new = _post_block(z_ref, h_ref, p_ref, wout_ref, gain_ref, wgate_ref, wproj_ref)
    o_ref[...] = h_new
    u = _rms_norm(h_new, gain1_ref[...]).astype(BF16)
    q_ref[...] = _dot(u, wq_ref[...]).astype(BF16)
    fl_ref[...] = _dot(u, wf_ref[...])
    v_ref[...] = _dot(u, wi_ref[...]).astype(BF16)
    g = _dot(u, wg_ref[...])
    gs_ref[...] = (g * _sigmoid(g)).astype(BF16)


def _post_rec_proj(z, h, p, wout, gain, wgate, wproj, gain1, wq, wf, wi, wg):
    T, D = h.shape
    tm = ROW_TILE
    const = lambda shape: pl.BlockSpec(shape, lambda i: (0,) * len(shape))
    tok = lambda w: pl.BlockSpec((tm, w), lambda i: (i, 0))
    bf = jax.ShapeDtypeStruct((T, D), BF16)
    f32 = jax.ShapeDtypeStruct((T, D), F32)
    return pl.pallas_call(
        _post_rec_proj_kernel,
        grid=(T // tm,),
        in_specs=[tok(D), tok(D), tok(PLE_DIM), const((D, D)), const((1, D)), const((D, D)),
                  const((PLE_DIM, D)), const((1, D)), const((D, D)), const((D, D)), const((D, D)),
                  const((D, D))],
        out_specs=[tok(D)] * 5,
        out_shape=[f32, bf, f32, bf, bf],
        compiler_params=pltpu.CompilerParams(
            dimension_semantics=("arbitrary",), vmem_limit_bytes=VMEM_LIMIT),
    )(z, h, p, wout, gain, wgate, wproj, gain1, wq, wf, wi, wg)


def _rec_levels():
    h = REC_CHUNK // 4
    out = []
    while h >= 1:
        out.append(h)
        h //= 2
    return out


def _rec_kernel(q_ref, fl_ref, v_ref, gs_ref, lb_ref, gain_ref, tril_ref, lvl_ref, z_ref,
                st_ref, b_scr, kb_scr):
    C = REC_CHUNK
    HALF = C // 2
    nq = C // REC_QUARTER

    @pl.when(pl.program_id(2) == 0)
    def _():
        st_ref[...] = jnp.zeros_like(st_ref)

    halves = (slice(0, HALF), slice(HALF, C))
    tril = tril_ref[...]

    def quarter_starts(b):
        ends = b.reshape(nq, REC_QUARTER, LANES)[:, REC_QUARTER - 1:, :]
        return jnp.concatenate([jnp.zeros_like(ends[:1]), ends[:-1]], axis=0), ends

    worst = jnp.zeros((1, LANES), F32)
    for hd in range(REC_HEADS_PER_STEP):
        lanes = slice(hd * REC_DIM, (hd + 1) * REC_DIM)
        lbs = lb_ref[:, lanes]
        e = jnp.exp(lbs - jnp.max(lbs, axis=0, keepdims=True))
        sm = e / jnp.sum(e, axis=0, keepdims=True)
        lb = (sm[0:1] + sm[1:2]) - sm[0:1]
        fl = fl_ref[0, :, lanes]
        log_f = jnp.log(lb + (1.0 - lb) * _sigmoid(fl))
        kb_scr[hd] = ((1.0 - lb) * _sigmoid(-fl)).astype(BF16)
        b = sum(_dot(tril, piece) for piece in _split_bf16(log_f, REC_PIECES)) * LOG2_E
        b_scr[hd] = b
        starts, ends = quarter_starts(b)
        worst = jnp.maximum(worst, jnp.max(starts - ends, axis=0))
    mild = jnp.max(worst) <= REC_MILD_DECAY

    def scores_and_output(hd, intra_quarter):
        lanes = slice(hd * REC_DIM, (hd + 1) * REC_DIM)
        b = b_scr[hd]
        qb = q_ref[0, :, lanes]
        kb = kb_scr[hd]
        v = v_ref[0, :, lanes]
        lvl = lvl_ref[...]

        def level_operands(ref):
            w = jnp.exp2(-jnp.abs(b - ref)).astype(BF16)
            return qb * w, kb * w

        def block_ref(h):
            b3 = b.reshape(C // (2 * h), 2 * h, LANES)
            return jnp.broadcast_to(b3[:, h - 1:h, :], b3.shape).reshape(C, LANES)

        levels = _rec_levels()
        if intra_quarter:
            starts, _ = quarter_starts(b)
            d = b - jnp.broadcast_to(starts, (nq, REC_QUARTER, LANES)).reshape(C, LANES)
            q_l, k_l = qb * jnp.exp2(d).astype(BF16), kb * jnp.exp2(-d).astype(BF16)
            inside = (lvl == 0) | (lvl > levels.index(REC_QUARTER) + 1)
            a_diag = [jnp.where(inside, _dot_nt(q_l[r], k_l[r]), 0.0) for r in halves]
            levels = [h for h in levels if h >= REC_QUARTER]
        else:
            a_diag = [jnp.where(lvl == 0, _dot_nt(qb[r], kb[r]), 0.0) for r in halves]
        row = lax.broadcasted_iota(jnp.int32, (C, LANES), 0)
        for h in levels:
            if h >= 4:
                ref = block_ref(h)
            elif h == 2:
                o4 = row & 3
                ref = jnp.where(o4 == 0, pltpu.roll(b, C - 1, 0),
                                jnp.where(o4 == 1, b, jnp.where(o4 == 2, pltpu.roll(b, 1, 0),
                                                                pltpu.roll(b, 2, 0))))
            else:
                ref = jnp.where((row & 1) == 0, b, pltpu.roll(b, 1, 0))
            q_l, k_l = level_operands(ref)
            own = lvl == _rec_levels().index(h) + 1
            a_diag = [jnp.where(own, _dot_nt(q_l[r], k_l[r]), a) for r, a in zip(halves, a_diag)]
        q_l, k_l = level_operands(b[HALF - 1:HALF, :])
        a_cross = _dot_nt(q_l[halves[1]], k_l[halves[0]])
        a = jnp.concatenate([jnp.concatenate([a_diag[0], jnp.zeros_like(a_cross)], axis=1),
                             jnp.concatenate([a_cross, a_diag[1]], axis=1)], axis=0)

        st = st_ref[hd]
        q_dec = qb * jnp.exp2(b).astype(BF16)
        o = _dot(a.astype(BF16), v) + _dot_nt(q_dec, st.astype(BF16))
        b_last = b[C - 1:C, :]
        k_dec = kb * jnp.exp2(b_last - b).astype(BF16)
        st_ref[hd] = st * jnp.exp2(b_last) + _dot_tn(v, k_dec)

        o = o * lax.rsqrt(jnp.mean(o * o, axis=-1, keepdims=True) + EPS) * gain_ref[...]
        z_ref[0, :, lanes] = (o * gs_ref[0, :, lanes].astype(F32)).astype(BF16)

    @pl.when(mild)
    def _():
        for hd in range(REC_HEADS_PER_STEP):
            scores_and_output(hd, intra_quarter=True)

    @pl.when(jnp.logical_not(mild))
    def _():
        for hd in range(REC_HEADS_PER_STEP):
            scores_and_output(hd, intra_quarter=False)


def _rec_level_map():
    size = REC_CHUNK // 2
    t = np.arange(size)[:, None]
    s = np.arange(size)[None, :]
    out = np.full((size, size), -1, np.int32)
    out[t == s] = 0
    for n, h in enumerate(_rec_levels()):
        own = (t // (2 * h) == s // (2 * h)) & ((t & h) != 0) & ((s & h) == 0)
        out[own] = n + 1
    return out


def _rec(q, fl, v, gs, rec_lb, gain, tril, lvl):
    B, S, D = q.shape
    C = REC_CHUNK
    width = REC_HEADS_PER_STEP * REC_DIM
    tok = pl.BlockSpec((1, C, width), lambda b, h, i: (b, i, h))
    const = lambda shape: pl.BlockSpec(shape, lambda b, h, i: (0,) * len(shape))
    return pl.pallas_call(
        _rec_kernel,
        grid=(B, REC_HEADS // REC_HEADS_PER_STEP, S // C),
        in_specs=[tok, tok, tok, tok,
                  pl.BlockSpec((rec_lb.shape[0], width), lambda b, h, i: (0, h)),
                  const((1, REC_DIM)), const((C, C)), const((C // 2, C // 2))],
        out_specs=tok,
        out_shape=jax.ShapeDtypeStruct((B, S, D), BF16),
        scratch_shapes=[pltpu.VMEM((REC_HEADS_PER_STEP, REC_DIM, REC_DIM), F32),
                        pltpu.VMEM((REC_HEADS_PER_STEP, C, REC_DIM), F32),
                        pltpu.VMEM((REC_HEADS_PER_STEP, C, REC_DIM), BF16)],
        compiler_params=pltpu.CompilerParams(
            dimension_semantics=("arbitrary", "arbitrary", "arbitrary"), vmem_limit_bytes=VMEM_LIMIT),
    )(q, fl, v, gs, rec_lb, gain, tril, lvl)


def _bias_lane(head, piece):
    return BIAS_LANES_PER_HEAD * head + piece


def _bias_selector():
    sel = np.zeros((C_PIECES, LANES, LANES), np.float32)
    for head in range(ATT_HEADS):
        for i in range(C_PIECES):
            sel[i, head, _bias_lane(head, i)] = 1.0
    return sel


def kernel(x, p, norm_pre, norm_post, att_w_in, att_b_f, att_w_out, rec_w_in, rec_lb, rec_out_norm,
           rec_w_out, ple_w_proj, ple_w_gate):
    B, S, D = x.shape
    assert D == D_MODEL and ROW_TILE == ATT_BLOCK and Q_TILES == 2
    assert S % (Q_GROUP * ATT_BLOCK) == 0 and S % REC_CHUNK == 0
    assert norm_pre.shape[0] == 2 and rec_lb.shape[0] == 2
    T = B * S
    W = ATT_HEADS * ATT_HEAD_DIM

    w_in = att_w_in[0]
    wq, wk, wv, wg = (w_in[:, n * W:(n + 1) * W].astype(BF16) for n in range(4))
    wfl = jnp.pad(w_in[:, 4 * W:], ((0, 0), (0, LANES - ATT_HEADS))).astype(BF16)
    bf = jnp.pad(att_b_f[0], (0, LANES - ATT_HEADS)).reshape(1, LANES)
    tril_row = jnp.asarray(np.tril(np.ones((ROW_TILE, ROW_TILE), np.float32)), BF16)
    sel = jnp.asarray(_bias_selector(), BF16)
    hsel = jnp.asarray(np.arange(D)[:, None] // ATT_HEAD_DIM == np.arange(LANES)[None, :], BF16)
    qt, k, vt, gs, caug, nblk = _fox_proj(x, norm_pre[0:1], wq, wk, wv.T, wg, wfl, bf, tril_row, sel,
                                          hsel)
    nblk = jnp.transpose(nblk[:, :, 0, :ATT_HEADS], (0, 2, 1)).reshape(-1)
    z = _fox_attn(nblk, qt, k, caug, vt, gs)
    r_in = rec_w_in[0]
    RW = REC_HEADS * REC_DIM
    rq, rf, ri, rg = (r_in[:, n * RW:(n + 1) * RW].astype(BF16) for n in range(4))
    h, q1, fl1, v1, gs1 = _post_rec_proj(
        z.reshape(T, D), x.reshape(T, D), p[0].reshape(T, PLE_DIM), att_w_out[0].astype(BF16),
        norm_post[0:1], ple_w_gate[0].astype(BF16), ple_w_proj[0].astype(BF16),
        norm_pre[1:2], rq, rf, ri, rg)
    tril_rec = jnp.asarray(np.tril(np.ones((REC_CHUNK, REC_CHUNK), np.float32)), BF16)
    lvl = jnp.asarray(_rec_level_map())
    shape3 = lambda a: a.reshape(B, S, D)
    z1 = _rec(shape3(q1), shape3(fl1), shape3(v1), shape3(gs1), rec_lb, rec_out_norm[0:1],
              tril_rec, lvl)
    out = _post(z1.reshape(T, D), h, p[1].reshape(T, PLE_DIM), rec_w_out[0].astype(BF16),
                norm_post[1:2], ple_w_gate[1].astype(BF16), ple_w_proj[1].astype(BF16))
    return out.reshape(B, S, D)
```

```python
import numpy as np
import jax
import jax.numpy as jnp
from jax import lax
from jax.experimental import pallas as pl
from jax.experimental.pallas import tpu as pltpu

F32 = jnp.float32
BF16 = jnp.bfloat16

D_MODEL = 1024
PLE_DIM = 256
ATT_HEADS = 16
ATT_HEAD_DIM = 64
REC_HEADS = 8
REC_DIM = 128
EPS = 1e-6
LOG2_E = 1.4426950408889634
SKIP_LOG2 = 130.0
NORM_MARGIN = 1.01
DOT_MARGIN = 2.0 ** -7

LANES = 128
HEAD_PAIRS = ATT_HEADS // 2
C_PIECES = 3
BIAS_LANES_PER_HEAD = 4

ROW_TILE = 512
ATT_BLOCK = 512
Q_TILES = 2
Q_GROUP = 8
COL_TILE = 256
M_INIT = -1e30
REC_CHUNK = 256
REC_HEADS_PER_STEP = 8
REC_PIECES = 2
REC_QUARTER = 64
REC_MILD_DECAY = 100.0
VMEM_LIMIT = 56 * 1024 * 1024

NT_DIMS = (((1,), (1,)), ((), ()))
TN_DIMS = (((0,), (0,)), ((), ()))


def _dot(a, b):
    return jnp.dot(a, b, preferred_element_type=F32)


def _dot_nt(a, b):
    return lax.dot_general(a, b, NT_DIMS, preferred_element_type=F32)


def _dot_tn(a, b):
    return lax.dot_general(a, b, TN_DIMS, preferred_element_type=F32)


def _split_bf16(x, pieces):
    out = []
    r = x
    for _ in range(pieces - 1):
        p = r.astype(BF16)
        out.append(p)
        r = r - p.astype(F32)
    out.append(r.astype(BF16))
    return out


def _rms_norm(x, gain):
    ms = jnp.mean(x * x, axis=-1, keepdims=True)
    return x * lax.rsqrt(ms + EPS) * gain


def _sigmoid(x):
    return 1.0 / (1.0 + jnp.exp(-x))


def _fox_proj_kernel(x_ref, gain_ref, wq_ref, wk_ref, wvt_ref, wg_ref, wfl_ref, bf_ref,
                     tril_ref, sel_ref, hsel_ref, qt_ref, k_ref, vt_ref, gs_ref, caug_ref, nblk_ref,
                     carry_ref, kmax_tab_ref, clast_tab_ref):
    tm = x_ref.shape[1]
    i = pl.program_id(1)
    u = _rms_norm(x_ref[0], gain_ref[...]).astype(BF16)
    qb = (_dot(u, wq_ref[...]) * (ATT_HEAD_DIM ** -0.5 * LOG2_E)).astype(BF16)
    kb = _dot(u, wk_ref[...]).astype(BF16)
    qt_ref[0, 0] = qb.T
    k_ref[0] = kb
    vt_ref[0, 0] = _dot_nt(wvt_ref[...], u).astype(BF16)
    g = _dot(u, wg_ref[...])
    gs_ref[0] = (g * _sigmoid(g)).astype(BF16)

    fl = _dot(u, wfl_ref[...]) + bf_ref[...]
    log_f = jnp.minimum(fl, 0.0) - jnp.log1p(jnp.exp(-jnp.abs(fl)))
    tril = tril_ref[...]
    c_blk = sum(_dot(tril, piece) for piece in _split_bf16(log_f, C_PIECES))

    @pl.when(i == 0)
    def _():
        carry_ref[...] = jnp.zeros_like(carry_ref)
        kmax_tab_ref[...] = jnp.zeros_like(kmax_tab_ref)
        clast_tab_ref[...] = jnp.zeros_like(clast_tab_ref)

    c = c_blk + carry_ref[...]
    carry_ref[...] = c[tm - 1:tm, :]
    c2 = c * LOG2_E
    caug = sum(_dot(piece, sel_ref[n]) for n, piece in enumerate(_split_bf16(-c2, C_PIECES)))
    caug_ref[0] = caug.astype(BF16)

    qf, kf = qb.astype(F32), kb.astype(F32)
    hsel = hsel_ref[...]
    per_head = lambda t: _dot(t.astype(BF16), hsel)
    qmax = jnp.sqrt(jnp.max(per_head(qf * qf), axis=0, keepdims=True)) * NORM_MARGIN
    kmax = jnp.sqrt(jnp.max(per_head(kf * kf), axis=0, keepdims=True)) * NORM_MARGIN
    m_low = jnp.min(per_head(qf * kf) - c2, axis=0, keepdims=True) - DOT_MARGIN * qmax * kmax
    kmax_tab_ref[pl.ds(i, 1), :] = kmax
    clast_tab_ref[pl.ds(i, 1), :] = c2[tm - 1:tm, :]
    blk = lax.broadcasted_iota(jnp.int32, kmax_tab_ref.shape, 0)
    bound = qmax * kmax_tab_ref[...] - clast_tab_ref[...] - m_low
    first_kept = jnp.min(jnp.where((blk < i) & (bound >= -SKIP_LOG2), blk, i), axis=0, keepdims=True)
    nblk_ref[0, 0] = jnp.broadcast_to(i - first_kept + 1, nblk_ref.shape[2:])


def _fox_proj(x, gain, wq, wk, wvt, wg, wfl, bf, tril, sel, hsel):
    B, S, D = x.shape
    tm = ROW_TILE
    nb = S // tm
    const = lambda shape: pl.BlockSpec(shape, lambda b, i: (0,) * len(shape))
    tok = pl.BlockSpec((1, tm, D), lambda b, i: (b, i, 0))
    out_tok = jax.ShapeDtypeStruct((B, S, D), BF16)
    tiled_t = pl.BlockSpec((1, 1, D, tm), lambda b, i: (b, i, 0, 0))
    out_t = jax.ShapeDtypeStruct((B, nb, D, tm), BF16)
    return pl.pallas_call(
        _fox_proj_kernel,
        grid=(B, nb),
        in_specs=[tok, const((1, D)), const((D, D)), const((D, D)), const((D, D)), const((D, D)),
                  const((D, LANES)), const((1, LANES)), const((tm, tm)), const((C_PIECES, LANES, LANES)),
                  const((D, LANES))],
        out_specs=[tiled_t, tok, tiled_t, tok, pl.BlockSpec((1, tm, LANES), lambda b, i: (b, i, 0)),
                   pl.BlockSpec((1, 1, 8, LANES), lambda b, i: (b, i, 0, 0))],
        out_shape=[out_t, out_tok, out_t, out_tok, jax.ShapeDtypeStruct((B, S, LANES), BF16),
                   jax.ShapeDtypeStruct((B, nb, 8, LANES), jnp.int32)],
        scratch_shapes=[pltpu.VMEM((1, LANES), F32), pltpu.VMEM((nb, LANES), F32),
                        pltpu.VMEM((nb, LANES), F32)],
        compiler_params=pltpu.CompilerParams(
            dimension_semantics=("arbitrary", "arbitrary"), vmem_limit_bytes=VMEM_LIMIT),
    )(x, gain, wq, wk, wvt, wg, wfl, bf, tril, sel, hsel)


def _fox_attn_kernel(nblk_ref, qt_ref, k_ref, caug_ref, vt_ref, gs_ref, z_ref,
                     s0_scr, s1_scr, s2_scr, s3_scr, mx0_scr, mx1_scr, mx2_scr, mx3_scr, m_scr, acc_scr):
    bk = ATT_BLOCK
    bq = Q_TILES * ATT_BLOCK
    b, pair, grp = pl.program_id(0), pl.program_id(1), pl.program_id(2)
    nb = pl.num_programs(2) * Q_GROUP
    ones_rows = jnp.ones((16, bk), BF16)
    bufs = ((s0_scr, mx0_scr), (s1_scr, mx1_scr), (s2_scr, mx2_scr), (s3_scr, mx3_scr))
    bias_row = lax.broadcasted_iota(jnp.int32, (LANES, bq), 0)
    no_head = jnp.zeros((ATT_HEAD_DIM, bq), BF16)

    def query_unit(sub, carry):
        unit = grp * (Q_GROUP // Q_TILES) + sub
        last = Q_TILES * unit + Q_TILES - 1
        q_t = jnp.concatenate([qt_ref[0, Q_TILES * sub + n] for n in range(Q_TILES)], axis=1)
        outs = []
        for hh in range(2):
            lo = hh * ATT_HEAD_DIM
            first_lane = BIAS_LANES_PER_HEAD * (2 * pair + hh)
            pick = jnp.where((bias_row >= first_lane) & (bias_row < first_lane + C_PIECES),
                             1.0, 0.0).astype(BF16)
            parts = [q_t[:ATT_HEAD_DIM], no_head] if hh == 0 else [no_head, q_t[ATT_HEAD_DIM:]]
            qt_aug = jnp.concatenate(parts + [pick], axis=0)

            def put_scores(j, dst, cols=slice(0, bq), diagonal=False):
                rows = pl.ds(pl.multiple_of(j * bk, bk), bk)
                k_aug = jnp.concatenate([k_ref[0, rows, :], caug_ref[0, rows, :]], axis=1)
                s_t = _dot(k_aug, qt_aug[:, cols])
                if diagonal:
                    shape = (bk, cols.stop - cols.start)
                    key_pos = j * bk + lax.broadcasted_iota(jnp.int32, shape, 0)
                    qry_pos = unit * bq + cols.start + lax.broadcasted_iota(jnp.int32, shape, 1)
                    s_t = jnp.where(key_pos <= qry_pos, s_t, -jnp.inf)
                dst[0][:, cols] = s_t
                dst[1][:, cols] = jnp.max(s_t, axis=0, keepdims=True)

            def update(j, src, cols=slice(0, bq)):
                m = m_scr[:, cols]
                m_new = jnp.maximum(m, src[1][:, cols])
                alpha = jnp.exp2(m - m_new)
                p_t = jnp.exp2((src[0][:, cols] - m_new).astype(BF16))
                v_aug = jnp.concatenate([vt_ref[0, j, lo:lo + ATT_HEAD_DIM, :], ones_rows], axis=0)
                acc_scr[:, cols] = alpha * acc_scr[:, cols] + _dot(v_aug, p_t)
                m_scr[:, cols] = m_new

            def run(*ops):
                split = [[slice(c, c + COL_TILE) for c in range(op[3].start, op[3].stop, COL_TILE)]
                         for op in ops]
                for n in range(max(len(tiles) for tiles in split)):
                    for (kind, j, buf, _, diagonal), tiles in zip(ops, split):
                        if n < len(tiles):
                            if kind == "put":
                                put_scores(j, buf, tiles[n], diagonal)
                            else:
                                update(j, buf, tiles[n])

            full = slice(0, bq)
            put = lambda j, buf, cols=full, diagonal=False: ("put", j, buf, cols, diagonal)
            upd = lambda j, buf, cols=full: ("upd", j, buf, cols, False)

            head_row = (b * ATT_HEADS + 2 * pair + hh) * nb
            n_steps = nblk_ref[head_row + last]
            for n in range(1, Q_TILES):
                n_steps = jnp.maximum(n_steps, nblk_ref[head_row + last - n] + n)
            m_scr[...] = jnp.full(m_scr.shape, M_INIT, F32)
            acc_scr[...] = jnp.zeros(acc_scr.shape, F32)
            last_tile = slice(bq - bk, bq)
            run(put(last, bufs[0], last_tile, True))
            run(put(last - 1, bufs[1], full, True))
            run(put(jnp.maximum(last - 2, 0), bufs[2]), upd(last, bufs[0], last_tile))

            def four_steps(g, c):
                t0 = 1 + 4 * g
                for n in range(4):
                    run(put(jnp.maximum(last - (t0 + n + 2), 0), bufs[(n + 3) % 4]),
                        upd(last - (t0 + n), bufs[(n + 1) % 4]))
                return c

            groups = (n_steps - 1) // 4
            lax.fori_loop(0, groups, four_steps, 0)
            t0 = 1 + 4 * groups
            rest = n_steps - t0

            @pl.when(rest == 1)
            def _():
                run(upd(last - t0, bufs[1]))

            @pl.when(rest == 2)
            def _():
                run(upd(last - t0, bufs[1]), upd(last - t0 - 1, bufs[2]))

            @pl.when(rest == 3)
            def _():
                run(put(last - t0 - 2, bufs[3]), upd(last - t0, bufs[1]))
                run(upd(last - t0 - 1, bufs[2]), upd(last - t0 - 2, bufs[3]))

            acc = acc_scr[...]
            outs.append(acc[:ATT_HEAD_DIM, :] / acc[ATT_HEAD_DIM:ATT_HEAD_DIM + 1, :])

        o = jnp.concatenate(outs, axis=0).T
        rows = pl.ds(pl.multiple_of(sub * bq, bq), bq)
        z_ref[0, rows, :] = (o * gs_ref[0, rows, :].astype(F32)).astype(BF16)
        return carry

    lax.fori_loop(0, Q_GROUP // Q_TILES, query_unit, 0)


def _fox_attn(nblk, qt, k, caug, vt, gs):
    B, S, D = k.shape
    bk = ATT_BLOCK
    bq = Q_TILES * ATT_BLOCK
    nb = S // bk
    tok = pl.BlockSpec((1, Q_GROUP * bk, LANES), lambda b, p, g, n: (b, g, p))
    seq = pl.BlockSpec((1, S, LANES), lambda b, p, g, n: (b, 0, p))
    return pl.pallas_call(
        _fox_attn_kernel,
        grid_spec=pltpu.PrefetchScalarGridSpec(
            num_scalar_prefetch=1,
            grid=(B, HEAD_PAIRS, nb // Q_GROUP),
            in_specs=[pl.BlockSpec((1, Q_GROUP, LANES, bk), lambda b, p, g, n: (b, g, p, 0)),
                      seq, pl.BlockSpec((1, S, LANES), lambda b, p, g, n: (b, 0, 0)),
                      pl.BlockSpec((1, nb, LANES, vt.shape[3]), lambda b, p, g, n: (b, 0, p, 0)),
                      tok],
            out_specs=tok,
            scratch_shapes=[pltpu.VMEM((bk, bq), F32)] * 4 + [pltpu.VMEM((1, bq), F32)] * 4
            + [pltpu.VMEM((1, bq), F32), pltpu.VMEM((ATT_HEAD_DIM + 16, bq), F32)]),
        out_shape=jax.ShapeDtypeStruct((B, S, D), BF16),
        compiler_params=pltpu.CompilerParams(
            dimension_semantics=("arbitrary", "arbitrary", "arbitrary"), vmem_limit_bytes=VMEM_LIMIT),
    )(nblk, qt, k, caug, vt, gs)


def _post_block(z_ref, h_ref, p_ref, wout_ref, gain_ref, wgate_ref, wproj_ref):
    y = _dot(z_ref[...], wout_ref[...])
    h1 = h_ref[...] + _rms_norm(y, gain_ref[...])
    gate = _sigmoid(_dot(h1.astype(BF16), wgate_ref[...]))
    return h1 + _dot(p_ref[...].astype(BF16), wproj_ref[...]) * gate


def _post_kernel(z_ref, h_ref, p_ref, wout_ref, gain_ref, wgate_ref, wproj_ref, o_ref):
    o_ref[...] = _post_block(z_ref, h_ref, p_ref, wout_ref, gain_ref, wgate_ref, wproj_ref)


def _post(z, h, p, wout, gain, wgate, wproj):
    T, D = h.shape
    tm = ROW_TILE
    const = lambda shape: pl.BlockSpec(shape, lambda i: (0,) * len(shape))
    tok = lambda w: pl.BlockSpec((tm, w), lambda i: (i, 0))
    return pl.pallas_call(
        _post_kernel,
        grid=(T // tm,),
        in_specs=[tok(D), tok(D), tok(PLE_DIM), const((D, D)), const((1, D)), const((D, D)),
                  const((PLE_DIM, D))],
        out_specs=tok(D),
        out_shape=jax.ShapeDtypeStruct((T, D), F32),
        compiler_params=pltpu.CompilerParams(
            dimension_semantics=("arbitrary",), vmem_limit_bytes=VMEM_LIMIT),
    )(z, h, p, wout, gain, wgate, wproj)


def _post_rec_proj_kernel(z_ref, h_ref, p_ref, wout_ref, gain_ref, wgate_ref, wproj_ref,
                          gain1_ref, wq_ref, wf_ref, wi_ref, wg_ref,
                          o_ref, q_ref, fl_ref, v_ref, gs_ref):
    h_new = _post_block(z_ref, h_ref, p_ref, wout_ref, gain_ref, wgate_ref, wproj_ref)
    o_ref[...] = h_new
    u = _rms_norm(h_new, gain1_ref[...]).astype(BF16)
    q_ref[...] = _dot(u, wq_ref[...]).astype(BF16)
    fl_ref[...] = _dot(u, wf_ref[...])
    v_ref[...] = _dot(u, wi_ref[...]).astype(BF16)
    g = _dot(u, wg_ref[...])
    gs_ref[...] = (g * _sigmoid(g)).astype(BF16)


def _post_rec_proj(z, h, p, wout, gain, wgate, wproj, gain1, wq, wf, wi, wg):
    T, D = h.shape
    tm = ROW_TILE
    const = lambda shape: pl.BlockSpec(shape, lambda i: (0,) * len(shape))
    tok = lambda w: pl.BlockSpec((tm, w), lambda i: (i, 0))
    bf = jax.ShapeDtypeStruct((T, D), BF16)
    f32 = jax.ShapeDtypeStruct((T, D), F32)
    return pl.pallas_call(
        _post_rec_proj_kernel,
        grid=(T // tm,),
        in_specs=[tok(D), tok(D), tok(PLE_DIM), const((D, D)), const((1, D)), const((D, D)),
                  const((PLE_DIM, D)), const((1, D)), const((D, D)), const((D, D)), const((D, D)),
                  const((D, D))],
        out_specs=[tok(D)] * 5,
        out_shape=[f32, bf, f32, bf, bf],
        compiler_params=pltpu.CompilerParams(
            dimension_semantics=("arbitrary",), vmem_limit_bytes=VMEM_LIMIT),
    )(z, h, p, wout, gain, wgate, wproj, gain1, wq, wf, wi, wg)


def _rec_levels():
    h = REC_CHUNK // 4
    out = []
    while h >= 1:
        out.append(h)
        h //= 2
    return out


def _rec_kernel(q_ref, fl_ref, v_ref, gs_ref, lb_ref, gain_ref, tril_ref, lvl_ref, z_ref,
                st_ref, b_scr, kb_scr):
    C = REC_CHUNK
    HALF = C // 2
    nq = C // REC_QUARTER

    @pl.when(pl.program_id(2) == 0)
    def _():
        st_ref[...] = jnp.zeros_like(st_ref)

    halves = (slice(0, HALF), slice(HALF, C))
    tril = tril_ref[...]

    def quarter_starts(b):
        ends = b.reshape(nq, REC_QUARTER, LANES)[:, REC_QUARTER - 1:, :]
        return jnp.concatenate([jnp.zeros_like(ends[:1]), ends[:-1]], axis=0), ends

    worst = jnp.zeros((1, LANES), F32)
    for hd in range(REC_HEADS_PER_STEP):
        lanes = slice(hd * REC_DIM, (hd + 1) * REC_DIM)
        lbs = lb_ref[:, lanes]
        e = jnp.exp(lbs - jnp.max(lbs, axis=0, keepdims=True))
        sm = e / jnp.sum(e, axis=0, keepdims=True)
        lb = (sm[0:1] + sm[1:2]) - sm[0:1]
        fl = fl_ref[0, :, lanes]
        log_f = jnp.log(lb + (1.0 - lb) * _sigmoid(fl))
        kb_scr[hd] = ((1.0 - lb) * _sigmoid(-fl)).astype(BF16)
        b = sum(_dot(tril, piece) for piece in _split_bf16(log_f, REC_PIECES)) * LOG2_E
        b_scr[hd] = b
        starts, ends = quarter_starts(b)
        worst = jnp.maximum(worst, jnp.max(starts - ends, axis=0))
    mild = jnp.max(worst) <= REC_MILD_DECAY

    def scores_and_output(hd, intra_quarter):
        lanes = slice(hd * REC_DIM, (hd + 1) * REC_DIM)
        b = b_scr[hd]
        qb = q_ref[0, :, lanes]
        kb = kb_scr[hd]
        v = v_ref[0, :, lanes]
        lvl = lvl_ref[...]

        def level_operands(ref):
            w = jnp.exp2(-jnp.abs(b - ref)).astype(BF16)
            return qb * w, kb * w

        def block_ref(h):
            b3 = b.reshape(C // (2 * h), 2 * h, LANES)
            return jnp.broadcast_to(b3[:, h - 1:h, :], b3.shape).reshape(C, LANES)

        levels = _rec_levels()
        if intra_quarter:
            starts, _ = quarter_starts(b)
            d = b - jnp.broadcast_to(starts, (nq, REC_QUARTER, LANES)).reshape(C, LANES)
            q_l, k_l = qb * jnp.exp2(d).astype(BF16), kb * jnp.exp2(-d).astype(BF16)
            inside = (lvl == 0) | (lvl > levels.index(REC_QUARTER) + 1)
            a_diag = [jnp.where(inside, _dot_nt(q_l[r], k_l[r]), 0.0) for r in halves]
            levels = [h for h in levels if h >= REC_QUARTER]
        else:
            a_diag = [jnp.where(lvl == 0, _dot_nt(qb[r], kb[r]), 0.0) for r in halves]
        row = lax.broadcasted_iota(jnp.int32, (C, LANES), 0)
        for h in levels:
            if h >= 4:
                ref = block_ref(h)
            elif h == 2:
                o4 = row & 3
                ref = jnp.where(o4 == 0, pltpu.roll(b, C - 1, 0),
                                jnp.where(o4 == 1, b, jnp.where(o4 == 2, pltpu.roll(b, 1, 0),
                                                                pltpu.roll(b, 2, 0))))
            else:
                ref = jnp.where((row & 1) == 0, b, pltpu.roll(b, 1, 0))
            q_l, k_l = level_operands(ref)
            own = lvl == _rec_levels().index(h) + 1
            a_diag = [jnp.where(own, _dot_nt(q_l[r], k_l[r]), a) for r, a in zip(halves, a_diag)]
        q_l, k_l = level_operands(b[HALF - 1:HALF, :])
        a_cross = _dot_nt(q_l[halves[1]], k_l[halves[0]])
        a = jnp.concatenate([jnp.concatenate([a_diag[0], jnp.zeros_like(a_cross)], axis=1),
                             jnp.concatenate([a_cross, a_diag[1]], axis=1)], axis=0)

        st = st_ref[hd]
        q_dec = qb * jnp.exp2(b).astype(BF16)
        o = _dot(a.astype(BF16), v) + _dot_nt(q_dec, st.astype(BF16))
        b_last = b[C - 1:C, :]
        k_dec = kb * jnp.exp2(b_last - b).astype(BF16)
        st_ref[hd] = st * jnp.exp2(b_last) + _dot_tn(v, k_dec)

        o = o * lax.rsqrt(jnp.mean(o * o, axis=-1, keepdims=True) + EPS) * gain_ref[...]
        z_ref[0, :, lanes] = (o * gs_ref[0, :, lanes].astype(F32)).astype(BF16)

    @pl.when(mild)
    def _():
        for hd in range(REC_HEADS_PER_STEP):
            scores_and_output(hd, intra_quarter=True)

    @pl.when(jnp.logical_not(mild))
    def _():
        for hd in range(REC_HEADS_PER_STEP):
            scores_and_output(hd, intra_quarter=False)


def _rec_level_map():
    size = REC_CHUNK // 2
    t = np.arange(size)[:, None]
    s = np.arange(size)[None, :]
    out = np.full((size, size), -1, np.int32)
    out[t == s] = 0
    for n, h in enumerate(_rec_levels()):
        own = (t // (2 * h) == s // (2 * h)) & ((t & h) != 0) & ((s & h) == 0)
        out[own] = n + 1
    return out


def _rec(q, fl, v, gs, rec_lb, gain, tril, lvl):
    B, S, D = q.shape
    C = REC_CHUNK
    width = REC_HEADS_PER_STEP * REC_DIM
    tok = pl.BlockSpec((1, C, width), lambda b, h, i: (b, i, h))
    const = lambda shape: pl.BlockSpec(shape, lambda b, h, i: (0,) * len(shape))
    return pl.pallas_call(
        _rec_kernel,
        grid=(B, REC_HEADS // REC_HEADS_PER_STEP, S // C),
        in_specs=[tok, tok, tok, tok,
                  pl.BlockSpec((rec_lb.shape[0], width), lambda b, h, i: (0, h)),
                  const((1, REC_DIM)), const((C, C)), const((C // 2, C // 2))],
        out_specs=tok,
        out_shape=jax.ShapeDtypeStruct((B, S, D), BF16),
        scratch_shapes=[pltpu.VMEM((REC_HEADS_PER_STEP, REC_DIM, REC_DIM), F32),
                        pltpu.VMEM((REC_HEADS_PER_STEP, C, REC_DIM), F32),
                        pltpu.VMEM((REC_HEADS_PER_STEP, C, REC_DIM), BF16)],
        compiler_params=pltpu.CompilerParams(
            dimension_semantics=("arbitrary", "arbitrary", "arbitrary"), vmem_limit_bytes=VMEM_LIMIT),
    )(q, fl, v, gs, rec_lb, gain, tril, lvl)


def _bias_lane(head, piece):
    return BIAS_LANES_PER_HEAD * head + piece


def _bias_selector():
    sel = np.zeros((C_PIECES, LANES, LANES), np.float32)
    for head in range(ATT_HEADS):
        for i in range(C_PIECES):
            sel[i, head, _bias_lane(head, i)] = 1.0
    return sel


def kernel(x, p, norm_pre, norm_post, att_w_in, att_b_f, att_w_out, rec_w_in, rec_lb, rec_out_norm,
           rec_w_out, ple_w_proj, ple_w_gate):
    B, S, D = x.shape
    assert D == D_MODEL and ROW_TILE == ATT_BLOCK and Q_TILES == 2
    assert S % (Q_GROUP * ATT_BLOCK) == 0 and S % REC_CHUNK == 0
    assert norm_pre.shape[0] == 2 and rec_lb.shape[0] == 2
    T = B * S
    W = ATT_HEADS * ATT_HEAD_DIM

    w_in = att_w_in[0]
    wq, wk, wv, wg = (w_in[:, n * W:(n + 1) * W].astype(BF16) for n in range(4))
    wfl = jnp.pad(w_in[:, 4 * W:], ((0, 0), (0, LANES - ATT_HEADS))).astype(BF16)
    bf = jnp.pad(att_b_f[0], (0, LANES - ATT_HEADS)).reshape(1, LANES)
    tril_row = jnp.asarray(np.tril(np.ones((ROW_TILE, ROW_TILE), np.float32)), BF16)
    sel = jnp.asarray(_bias_selector(), BF16)
    hsel = jnp.asarray(np.arange(D)[:, None] // ATT_HEAD_DIM == np.arange(LANES)[None, :], BF16)
    qt, k, vt, gs, caug, nblk = _fox_proj(x, norm_pre[0:1], wq, wk, wv.T, wg, wfl, bf, tril_row, sel,
                                          hsel)
    nblk = jnp.transpose(nblk[:, :, 0, :ATT_HEADS], (0, 2, 1)).reshape(-1)
    z = _fox_attn(nblk, qt, k, caug, vt, gs)
    r_in = rec_w_in[0]
    RW = REC_HEADS * REC_DIM
    rq, rf, ri, rg = (r_in[:, n * RW:(n + 1) * RW].astype(BF16) for n in range(4))
    h, q1, fl1, v1, gs1 = _post_rec_proj(
        z.reshape(T, D), x.reshape(T, D), p[0].reshape(T, PLE_DIM), att_w_out[0].astype(BF16),
        norm_post[0:1], ple_w_gate[0].astype(BF16), ple_w_proj[0].astype(BF16),
        norm_pre[1:2], rq, rf, ri, rg)
    tril_rec = jnp.asarray(np.tril(np.ones((REC_CHUNK, REC_CHUNK), np.float32)), BF16)
    lvl = jnp.asarray(_rec_level_map())
    shape3 = lambda a: a.reshape(B, S, D)
    z1 = _rec(shape3(q1), shape3(fl1), shape3(v1), shape3(gs1), rec_lb, rec_out_norm[0:1],
              tril_rec, lvl)
    out = _post(z1.reshape(T, D), h, p[1].reshape(T, PLE_DIM), rec_w_out[0].astype(BF16),
                norm_post[1:2], ple_w_gate[1].astype(BF16), ple_w_proj[1].astype(BF16))
    return out.reshape(B, S, D)
```

```python
import numpy as np
import jax
import jax.numpy as jnp
from jax import lax
from jax.experimental import pallas as pl
from jax.experimental.pallas import tpu as pltpu

F32 = jnp.float32
BF16 = jnp.bfloat16

D_MODEL = 1024
PLE_DIM = 256
ATT_HEADS = 16
ATT_HEAD_DIM = 64
REC_HEADS = 8
REC_DIM = 128
EPS = 1e-6
LOG2_E = 1.4426950408889634
SKIP_LOG2 = 130.0
NORM_MARGIN = 1.01
DOT_MARGIN = 2.0 ** -7

LANES = 128
HEAD_PAIRS = ATT_HEADS // 2
C_PIECES = 3
BIAS_LANES_PER_HEAD = 4

ROW_TILE = 512
ATT_BLOCK = 512
Q_TILES = 2
Q_GROUP = 8
COL_TILE = 256
M_INIT = -1e30
REC_CHUNK = 256
REC_HEADS_PER_STEP = 8
REC_PIECES = 2
REC_QUARTER = 64
REC_MILD_DECAY = 100.0
VMEM_LIMIT = 56 * 1024 * 1024

NT_DIMS = (((1,), (1,)), ((), ()))
TN_DIMS = (((0,), (0,)), ((), ()))


def _dot(a, b):
    return jnp.dot(a, b, preferred_element_type=F32)


def _dot_nt(a, b):
    return lax.dot_general(a, b, NT_DIMS, preferred_element_type=F32)


def _dot_tn(a, b):
    return lax.dot_general(a, b, TN_DIMS, preferred_element_type=F32)


def _split_bf16(x, pieces):
    out = []
    r = x
    for _ in range(pieces - 1):
        p = r.astype(BF16)
        out.append(p)
        r = r - p.astype(F32)
    out.append(r.astype(BF16))
    return out


def _rms_norm(x, gain):
    ms = jnp.mean(x * x, axis=-1, keepdims=True)
    return x * lax.rsqrt(ms + EPS) * gain


def _sigmoid(x):
    return 1.0 / (1.0 + jnp.exp(-x))


def _fox_proj_kernel(x_ref, gain_ref, wq_ref, wk_ref, wvt_ref, wg_ref, wfl_ref, bf_ref,
                     tril_ref, sel_ref, qt_ref, k_ref, vt_ref, gs_ref, caug_ref, nblk_ref,
                     carry_ref, kmax_tab_ref, clast_tab_ref):
    tm = x_ref.shape[1]
    i = pl.program_id(1)

    @pl.when(i == 0)
    def _():
        carry_ref[...] = jnp.zeros_like(carry_ref)
        kmax_tab_ref[...] = jnp.zeros_like(kmax_tab_ref)
        clast_tab_ref[...] = jnp.zeros_like(clast_tab_ref)

    u = _rms_norm(x_ref[0], gain_ref[...]).astype(BF16)

    qb = (_dot(u, wq_ref[...]) * (ATT_HEAD_DIM ** -0.5 * LOG2_E)).astype(BF16)
    kb = _dot(u, wk_ref[...]).astype(BF16)
    qt = qb.T
    qt_ref[0, 0] = qt
    k_ref[0] = kb
    qtf, ktf = qt.astype(F32), kb.T.astype(F32)
    per_head = lambda t: jnp.sum(t.reshape(ATT_HEADS, ATT_HEAD_DIM, tm), axis=1)
    qn2, kn2, qk = per_head(qtf * qtf), per_head(ktf * ktf), per_head(qtf * ktf)
    vt_ref[0, 0] = _dot_nt(wvt_ref[...], u).astype(BF16)
    g = _dot(u, wg_ref[...])
    gs_ref[0] = (g * _sigmoid(g)).astype(BF16)

    fl = _dot(u, wfl_ref[...]) + bf_ref[...]
    log_f = jnp.minimum(fl, 0.0) - jnp.log1p(jnp.exp(-jnp.abs(fl)))
    tril = tril_ref[...]
    c = sum(_dot(tril, piece) for piece in _split_bf16(log_f, C_PIECES)) + carry_ref[...]
    carry_ref[...] = c[tm - 1:tm, :]
    c2 = c * LOG2_E
    caug = sum(_dot(piece, sel_ref[n]) for n, piece in enumerate(_split_bf16(-c2, C_PIECES)))
    caug_ref[0] = caug.astype(BF16)

    c2_t = c2.T[:ATT_HEADS]
    qmax = jnp.sqrt(jnp.max(qn2, axis=1, keepdims=True)) * NORM_MARGIN
    kmax = jnp.sqrt(jnp.max(kn2, axis=1, keepdims=True)) * NORM_MARGIN
    m_low = jnp.min(qk - c2_t, axis=1, keepdims=True) - DOT_MARGIN * qmax * kmax
    blk = lax.broadcasted_iota(jnp.int32, kmax_tab_ref.shape, 1)
    kmax_tab = jnp.where(blk == i, kmax, kmax_tab_ref[...])
    clast_tab = jnp.where(blk == i, c2_t[:, tm - 1:tm], clast_tab_ref[...])
    kmax_tab_ref[...] = kmax_tab
    clast_tab_ref[...] = clast_tab
    bound = qmax * kmax_tab - clast_tab - m_low
    first_kept = jnp.min(jnp.where((blk < i) & (bound >= -SKIP_LOG2), blk, i), axis=1, keepdims=True)
    nblk_ref[0, 0] = jnp.broadcast_to(i - first_kept + 1, nblk_ref.shape[2:])


def _fox_proj(x, gain, wq, wk, wvt, wg, wfl, bf, tril, sel):
    B, S, D = x.shape
    tm = ROW_TILE
    nb = S // tm
    const = lambda shape: pl.BlockSpec(shape, lambda b, i: (0,) * len(shape))
    tok = pl.BlockSpec((1, tm, D), lambda b, i: (b, i, 0))
    out_tok = jax.ShapeDtypeStruct((B, S, D), BF16)
    tiled_t = pl.BlockSpec((1, 1, D, tm), lambda b, i: (b, i, 0, 0))
    out_t = jax.ShapeDtypeStruct((B, nb, D, tm), BF16)
    return pl.pallas_call(
        _fox_proj_kernel,
        grid=(B, nb),
        in_specs=[tok, const((1, D)), const((D, D)), const((D, D)), const((D, D)), const((D, D)),
                  const((D, LANES)), const((1, LANES)), const((tm, tm)), const((C_PIECES, LANES, LANES))],
        out_specs=[tiled_t, tok, tiled_t, tok, pl.BlockSpec((1, tm, LANES), lambda b, i: (b, i, 0)),
                   pl.BlockSpec((1, 1, ATT_HEADS, LANES), lambda b, i: (b, i, 0, 0))],
        out_shape=[out_t, out_tok, out_t, out_tok, jax.ShapeDtypeStruct((B, S, LANES), BF16),
                   jax.ShapeDtypeStruct((B, nb, ATT_HEADS, LANES), jnp.int32)],
        scratch_shapes=[pltpu.VMEM((1, LANES), F32), pltpu.VMEM((ATT_HEADS, LANES), F32),
                        pltpu.VMEM((ATT_HEADS, LANES), F32)],
        compiler_params=pltpu.CompilerParams(
            dimension_semantics=("arbitrary", "arbitrary"), vmem_limit_bytes=VMEM_LIMIT),
    )(x, gain, wq, wk, wvt, wg, wfl, bf, tril, sel)


def _fox_attn_kernel(nblk_ref, qt_ref, k_ref, caug_ref, vt_ref, gs_ref, z_ref,
                     s0_scr, s1_scr, s2_scr, s3_scr, mx0_scr, mx1_scr, mx2_scr, mx3_scr, m_scr, acc_scr):
    bk = ATT_BLOCK
    bq = Q_TILES * ATT_BLOCK
    b, pair, grp = pl.program_id(0), pl.program_id(1), pl.program_id(2)
    nb = pl.num_programs(2) * Q_GROUP
    ones_rows = jnp.ones((16, bk), BF16)
    bufs = ((s0_scr, mx0_scr), (s1_scr, mx1_scr), (s2_scr, mx2_scr), (s3_scr, mx3_scr))
    bias_row = lax.broadcasted_iota(jnp.int32, (LANES, bq), 0)
    no_head = jnp.zeros((ATT_HEAD_DIM, bq), BF16)

    def query_unit(sub, carry):
        unit = grp * (Q_GROUP // Q_TILES) + sub
        last = Q_TILES * unit + Q_TILES - 1
        q_t = jnp.concatenate([qt_ref[0, Q_TILES * sub + n] for n in range(Q_TILES)], axis=1)
        outs = []
        for hh in range(2):
            lo = hh * ATT_HEAD_DIM
            first_lane = BIAS_LANES_PER_HEAD * (2 * pair + hh)
            pick = jnp.where((bias_row >= first_lane) & (bias_row < first_lane + C_PIECES),
                             1.0, 0.0).astype(BF16)
            parts = [q_t[:ATT_HEAD_DIM], no_head] if hh == 0 else [no_head, q_t[ATT_HEAD_DIM:]]
            qt_aug = jnp.concatenate(parts + [pick], axis=0)

            def put_scores(j, dst, cols=slice(0, bq), diagonal=False):
                rows = pl.ds(pl.multiple_of(j * bk, bk), bk)
                k_aug = jnp.concatenate([k_ref[0, rows, :], caug_ref[0, rows, :]], axis=1)
                s_t = _dot(k_aug, qt_aug[:, cols])
                if diagonal:
                    shape = (bk, cols.stop - cols.start)
                    key_pos = j * bk + lax.broadcasted_iota(jnp.int32, shape, 0)
                    qry_pos = unit * bq + cols.start + lax.broadcasted_iota(jnp.int32, shape, 1)
                    s_t = jnp.where(key_pos <= qry_pos, s_t, -jnp.inf)
                dst[0][:, cols] = s_t
                dst[1][:, cols] = jnp.max(s_t, axis=0, keepdims=True)

            def update(j, src, cols=slice(0, bq)):
                m = m_scr[:, cols]
                m_new = jnp.maximum(m, src[1][:, cols])
                alpha = jnp.exp2(m - m_new)
                p_t = jnp.exp2((src[0][:, cols] - m_new).astype(BF16))
                v_aug = jnp.concatenate([vt_ref[0, j, lo:lo + ATT_HEAD_DIM, :], ones_rows], axis=0)
                acc_scr[:, cols] = alpha * acc_scr[:, cols] + _dot(v_aug, p_t)
                m_scr[:, cols] = m_new

            def run(*ops):
                split = [[slice(c, c + COL_TILE) for c in range(op[3].start, op[3].stop, COL_TILE)]
                         for op in ops]
                for n in range(max(len(tiles) for tiles in split)):
                    for (kind, j, buf, _, diagonal), tiles in zip(ops, split):
                        if n < len(tiles):
                            if kind == "put":
                                put_scores(j, buf, tiles[n], diagonal)
                            else:
                                update(j, buf, tiles[n])

            full = slice(0, bq)
            put = lambda j, buf, cols=full, diagonal=False: ("put", j, buf, cols, diagonal)
            upd = lambda j, buf, cols=full: ("upd", j, buf, cols, False)

            head_row = (b * ATT_HEADS + 2 * pair + hh) * nb
            n_steps = nblk_ref[head_row + last]
            for n in range(1, Q_TILES):
                n_steps = jnp.maximum(n_steps, nblk_ref[head_row + last - n] + n)
            m_scr[...] = jnp.full(m_scr.shape, M_INIT, F32)
            acc_scr[...] = jnp.zeros(acc_scr.shape, F32)
            last_tile = slice(bq - bk, bq)
            run(put(last, bufs[0], last_tile, True))
            run(put(last - 1, bufs[1], full, True))
            run(put(jnp.maximum(last - 2, 0), bufs[2]), upd(last, bufs[0], last_tile))

            def four_steps(g, c):
                t0 = 1 + 4 * g
                for n in range(4):
                    run(put(jnp.maximum(last - (t0 + n + 2), 0), bufs[(n + 3) % 4]),
                        upd(last - (t0 + n), bufs[(n + 1) % 4]))
                return c

            groups = (n_steps - 1) // 4
            lax.fori_loop(0, groups, four_steps, 0)
            t0 = 1 + 4 * groups
            rest = n_steps - t0

            @pl.when(rest == 1)
            def _():
                run(upd(last - t0, bufs[1]))

            @pl.when(rest == 2)
            def _():
                run(upd(last - t0, bufs[1]), upd(last - t0 - 1, bufs[2]))

            @pl.when(rest == 3)
            def _():
                run(put(last - t0 - 2, bufs[3]), upd(last - t0, bufs[1]))
                run(upd(last - t0 - 1, bufs[2]), upd(last - t0 - 2, bufs[3]))

            acc = acc_scr[...]
            outs.append(acc[:ATT_HEAD_DIM, :] / acc[ATT_HEAD_DIM:ATT_HEAD_DIM + 1, :])

        o = jnp.concatenate(outs, axis=0).T
        rows = pl.ds(pl.multiple_of(sub * bq, bq), bq)
        z_ref[0, rows, :] = (o * gs_ref[0, rows, :].astype(F32)).astype(BF16)
        return carry

    lax.fori_loop(0, Q_GROUP // Q_TILES, query_unit, 0)


def _fox_attn(nblk, qt, k, caug, vt, gs):
    B, S, D = k.shape
    bk = ATT_BLOCK
    bq = Q_TILES * ATT_BLOCK
    nb = S // bk
    tok = pl.BlockSpec((1, Q_GROUP * bk, LANES), lambda b, p, g, n: (b, g, p))
    seq = pl.BlockSpec((1, S, LANES), lambda b, p, g, n: (b, 0, p))
    return pl.pallas_call(
        _fox_attn_kernel,
        grid_spec=pltpu.PrefetchScalarGridSpec(
            num_scalar_prefetch=1,
            grid=(B, HEAD_PAIRS, nb // Q_GROUP),
            in_specs=[pl.BlockSpec((1, Q_GROUP, LANES, bk), lambda b, p, g, n: (b, g, p, 0)),
                      seq, pl.BlockSpec((1, S, LANES), lambda b, p, g, n: (b, 0, 0)),
                      pl.BlockSpec((1, nb, LANES, vt.shape[3]), lambda b, p, g, n: (b, 0, p, 0)),
                      tok],
            out_specs=tok,
            scratch_shapes=[pltpu.VMEM((bk, bq), F32)] * 4 + [pltpu.VMEM((1, bq), F32)] * 4
            + [pltpu.VMEM((1, bq), F32), pltpu.VMEM((ATT_HEAD_DIM + 16, bq), F32)]),
        out_shape=jax.ShapeDtypeStruct((B, S, D), BF16),
        compiler_params=pltpu.CompilerParams(
            dimension_semantics=("arbitrary", "arbitrary", "arbitrary"), vmem_limit_bytes=VMEM_LIMIT),
    )(nblk, qt, k, caug, vt, gs)


def _post_block(z_ref, h_ref, p_ref, wout_ref, gain_ref, wgate_ref, wproj_ref):
    y = _dot(z_ref[...], wout_ref[...])
    h1 = h_ref[...] + _rms_norm(y, gain_ref[...])
    gate = _sigmoid(_dot(h1.astype(BF16), wgate_ref[...]))
    return h1 + _dot(p_ref[...].astype(BF16), wproj_ref[...]) * gate


def _post_kernel(z_ref, h_ref, p_ref, wout_ref, gain_ref, wgate_ref, wproj_ref, o_ref):
    o_ref[...] = _post_block(z_ref, h_ref, p_ref, wout_ref, gain_ref, wgate_ref, wproj_ref)


def _post(z, h, p, wout, gain, wgate, wproj):
    T, D = h.shape
    tm = ROW_TILE
    const = lambda shape: pl.BlockSpec(shape, lambda i: (0,) * len(shape))
    tok = lambda w: pl.BlockSpec((tm, w), lambda i: (i, 0))
    return pl.pallas_call(
        _post_kernel,
        grid=(T // tm,),
        in_specs=[tok(D), tok(D), tok(PLE_DIM), const((D, D)), const((1, D)), const((D, D)),
                  const((PLE_DIM, D))],
        out_specs=tok(D),
        out_shape=jax.ShapeDtypeStruct((T, D), F32),
        compiler_params=pltpu.CompilerParams(
            dimension_semantics=("arbitrary",), vmem_limit_bytes=VMEM_LIMIT),
    )(z, h, p, wout, gain, wgate, wproj)


def _post_rec_proj_kernel(z_ref, h_ref, p_ref, wout_ref, gain_ref, wgate_ref, wproj_ref,
                          gain1_ref, wq_ref, wf_ref, wi_ref, wg_ref,
                          o_ref, q_ref, fl_ref, v_ref, gs_ref):
    h_new = _post_block(z_ref, h_ref, p_ref, wout_ref, gain_ref, wgate_ref, wproj_ref)
    o_ref[...] = h_new
    u = _rms_norm(h_new, gain1_ref[...]).astype(BF16)
    q_ref[...] = _dot(u, wq_ref[...]).astype(BF16)
    fl_ref[...] = _dot(u, wf_ref[...])
    v_ref[...] = _dot(u, wi_ref[...]).astype(BF16)
    g = _dot(u, wg_ref[...])
    gs_ref[...] = (g * _sigmoid(g)).astype(BF16)


def _post_rec_proj(z, h, p, wout, gain, wgate, wproj, gain1, wq, wf, wi, wg):
    T, D = h.shape
    tm = ROW_TILE
    const = lambda shape: pl.BlockSpec(shape, lambda i: (0,) * len(shape))
    tok = lambda w: pl.BlockSpec((tm, w), lambda i: (i, 0))
    bf = jax.ShapeDtypeStruct((T, D), BF16)
    f32 = jax.ShapeDtypeStruct((T, D), F32)
    return pl.pallas_call(
        _post_rec_proj_kernel,
        grid=(T // tm,),
        in_specs=[tok(D), tok(D), tok(PLE_DIM), const((D, D)), const((1, D)), const((D, D)),
                  const((PLE_DIM, D)), const((1, D)), const((D, D)), const((D, D)), const((D, D)),
                  const((D, D))],
        out_specs=[tok(D)] * 5,
        out_shape=[f32, bf, f32, bf, bf],
        compiler_params=pltpu.CompilerParams(
            dimension_semantics=("arbitrary",), vmem_limit_bytes=VMEM_LIMIT),
    )(z, h, p, wout, gain, wgate, wproj, gain1, wq, wf, wi, wg)


def _rec_levels():
    h = REC_CHUNK // 4
    out = []
    while h >= 1:
        out.append(h)
        h //= 2
    return out


def _rec_kernel(q_ref, fl_ref, v_ref, gs_ref, lb_ref, gain_ref, tril_ref, lvl_ref, z_ref,
                st_ref, b_scr, kb_scr):
    C = REC_CHUNK
    HALF = C // 2
    nq = C // REC_QUARTER

    @pl.when(pl.program_id(2) == 0)
    def _():
        st_ref[...] = jnp.zeros_like(st_ref)

    halves = (slice(0, HALF), slice(HALF, C))
    tril = tril_ref[...]

    def quarter_starts(b):
        ends = b.reshape(nq, REC_QUARTER, LANES)[:, REC_QUARTER - 1:, :]
        return jnp.concatenate([jnp.zeros_like(ends[:1]), ends[:-1]], axis=0), ends

    worst = jnp.zeros((1, LANES), F32)
    for hd in range(REC_HEADS_PER_STEP):
        lanes = slice(hd * REC_DIM, (hd + 1) * REC_DIM)
        lbs = lb_ref[:, lanes]
        e = jnp.exp(lbs - jnp.max(lbs, axis=0, keepdims=True))
        sm = e / jnp.sum(e, axis=0, keepdims=True)
        lb = (sm[0:1] + sm[1:2]) - sm[0:1]
        fl = fl_ref[0, :, lanes]
        log_f = jnp.log(lb + (1.0 - lb) * _sigmoid(fl))
        kb_scr[hd] = ((1.0 - lb) * _sigmoid(-fl)).astype(BF16)
        b = sum(_dot(tril, piece) for piece in _split_bf16(log_f, REC_PIECES)) * LOG2_E
        b_scr[hd] = b
        starts, ends = quarter_starts(b)
        worst = jnp.maximum(worst, jnp.max(starts - ends, axis=0))
    mild = jnp.max(worst) <= REC_MILD_DECAY

    def scores_and_output(hd, intra_quarter):
        lanes = slice(hd * REC_DIM, (hd + 1) * REC_DIM)
        b = b_scr[hd]
        qb = q_ref[0, :, lanes]
        kb = kb_scr[hd]
        v = v_ref[0, :, lanes]
        lvl = lvl_ref[...]

        def level_operands(ref):
            w = jnp.exp2(-jnp.abs(b - ref)).astype(BF16)
            return qb * w, kb * w

        def block_ref(h):
            b3 = b.reshape(C // (2 * h), 2 * h, LANES)
            return jnp.broadcast_to(b3[:, h - 1:h, :], b3.shape).reshape(C, LANES)

        levels = _rec_levels()
        if intra_quarter:
            starts, _ = quarter_starts(b)
            d = b - jnp.broadcast_to(starts, (nq, REC_QUARTER, LANES)).reshape(C, LANES)
            q_l, k_l = qb * jnp.exp2(d).astype(BF16), kb * jnp.exp2(-d).astype(BF16)
            inside = (lvl == 0) | (lvl > levels.index(REC_QUARTER) + 1)
            a_diag = [jnp.where(inside, _dot_nt(q_l[r], k_l[r]), 0.0) for r in halves]
            levels = [h for h in levels if h >= REC_QUARTER]
        else:
            a_diag = [jnp.where(lvl == 0, _dot_nt(qb[r], kb[r]), 0.0) for r in halves]
        row = lax.broadcasted_iota(jnp.int32, (C, LANES), 0)
        for h in levels:
            if h >= 4:
                ref = block_ref(h)
            elif h == 2:
                o4 = row & 3
                ref = jnp.where(o4 == 0, pltpu.roll(b, C - 1, 0),
                                jnp.where(o4 == 1, b, jnp.where(o4 == 2, pltpu.roll(b, 1, 0),
                                                                pltpu.roll(b, 2, 0))))
            else:
                ref = jnp.where((row & 1) == 0, b, pltpu.roll(b, 1, 0))
            q_l, k_l = level_operands(ref)
            own = lvl == _rec_levels().index(h) + 1
            a_diag = [jnp.where(own, _dot_nt(q_l[r], k_l[r]), a) for r, a in zip(halves, a_diag)]
        q_l, k_l = level_operands(b[HALF - 1:HALF, :])
        a_cross = _dot_nt(q_l[halves[1]], k_l[halves[0]])
        a = jnp.concatenate([jnp.concatenate([a_diag[0], jnp.zeros_like(a_cross)], axis=1),
                             jnp.concatenate([a_cross, a_diag[1]], axis=1)], axis=0)

        st = st_ref[hd]
        q_dec = qb * jnp.exp2(b).astype(BF16)
        o = _dot(a.astype(BF16), v) + _dot_nt(q_dec, st.astype(BF16))
        b_last = b[C - 1:C, :]
        k_dec = kb * jnp.exp2(b_last - b).astype(BF16)
        st_ref[hd] = st * jnp.exp2(b_last) + _dot_tn(v, k_dec)

        o = o * lax.rsqrt(jnp.mean(o * o, axis=-1, keepdims=True) + EPS) * gain_ref[...]
        z_ref[0, :, lanes] = (o * gs_ref[0, :, lanes].astype(F32)).astype(BF16)

    @pl.when(mild)
    def _():
        for hd in range(REC_HEADS_PER_STEP):
            scores_and_output(hd, intra_quarter=True)

    @pl.when(jnp.logical_not(mild))
    def _():
        for hd in range(REC_HEADS_PER_STEP):
            scores_and_output(hd, intra_quarter=False)


def _rec_level_map():
    size = REC_CHUNK // 2
    t = np.arange(size)[:, None]
    s = np.arange(size)[None, :]
    out = np.full((size, size), -1, np.int32)
    out[t == s] = 0
    for n, h in enumerate(_rec_levels()):
        own = (t // (2 * h) == s // (2 * h)) & ((t & h) != 0) & ((s & h) == 0)
        out[own] = n + 1
    return out


def _rec(q, fl, v, gs, rec_lb, gain, tril, lvl):
    B, S, D = q.shape
    C = REC_CHUNK
    width = REC_HEADS_PER_STEP * REC_DIM
    tok = pl.BlockSpec((1, C, width), lambda b, h, i: (b, i, h))
    const = lambda shape: pl.BlockSpec(shape, lambda b, h, i: (0,) * len(shape))
    return pl.pallas_call(
        _rec_kernel,
        grid=(B, REC_HEADS // REC_HEADS_PER_STEP, S // C),
        in_specs=[tok, tok, tok, tok,
                  pl.BlockSpec((rec_lb.shape[0], width), lambda b, h, i: (0, h)),
                  const((1, REC_DIM)), const((C, C)), const((C // 2, C // 2))],
        out_specs=tok,
        out_shape=jax.ShapeDtypeStruct((B, S, D), BF16),
        scratch_shapes=[pltpu.VMEM((REC_HEADS_PER_STEP, REC_DIM, REC_DIM), F32),
                        pltpu.VMEM((REC_HEADS_PER_STEP, C, REC_DIM), F32),
                        pltpu.VMEM((REC_HEADS_PER_STEP, C, REC_DIM), BF16)],
        compiler_params=pltpu.CompilerParams(
            dimension_semantics=("arbitrary", "arbitrary", "arbitrary"), vmem_limit_bytes=VMEM_LIMIT),
    )(q, fl, v, gs, rec_lb, gain, tril, lvl)


def _bias_lane(head, piece):
    return BIAS_LANES_PER_HEAD * head + piece


def _bias_selector():
    sel = np.zeros((C_PIECES, LANES, LANES), np.float32)
    for head in range(ATT_HEADS):
        for i in range(C_PIECES):
            sel[i, head, _bias_lane(head, i)] = 1.0
    return sel


def kernel(x, p, norm_pre, norm_post, att_w_in, att_b_f, att_w_out, rec_w_in, rec_lb, rec_out_norm,
           rec_w_out, ple_w_proj, ple_w_gate):
    B, S, D = x.shape
    assert D == D_MODEL and ROW_TILE == ATT_BLOCK and Q_TILES == 2
    assert S % (Q_GROUP * ATT_BLOCK) == 0 and S % REC_CHUNK == 0 and S // ATT_BLOCK <= LANES
    assert norm_pre.shape[0] == 2 and rec_lb.shape[0] == 2
    T = B * S
    W = ATT_HEADS * ATT_HEAD_DIM

    w_in = att_w_in[0]
    wq, wk, wv, wg = (w_in[:, n * W:(n + 1) * W].astype(BF16) for n in range(4))
    wfl = jnp.pad(w_in[:, 4 * W:], ((0, 0), (0, LANES - ATT_HEADS))).astype(BF16)
    bf = jnp.pad(att_b_f[0], (0, LANES - ATT_HEADS)).reshape(1, LANES)
    tril_row = jnp.asarray(np.tril(np.ones((ROW_TILE, ROW_TILE), np.float32)), BF16)
    sel = jnp.asarray(_bias_selector(), BF16)
    qt, k, vt, gs, caug, nblk = _fox_proj(x, norm_pre[0:1], wq, wk, wv.T, wg, wfl, bf, tril_row, sel)
    nblk = jnp.transpose(nblk[:, :, :, 0], (0, 2, 1)).reshape(-1)
    z = _fox_attn(nblk, qt, k, caug, vt, gs)
    r_in = rec_w_in[0]
    RW = REC_HEADS * REC_DIM
    rq, rf, ri, rg = (r_in[:, n * RW:(n + 1) * RW].astype(BF16) for n in range(4))
    h, q1, fl1, v1, gs1 = _post_rec_proj(
        z.reshape(T, D), x.reshape(T, D), p[0].reshape(T, PLE_DIM), att_w_out[0].astype(BF16),
        norm_post[0:1], ple_w_gate[0].astype(BF16), ple_w_proj[0].astype(BF16),
        norm_pre[1:2], rq, rf, ri, rg)
    tril_rec = jnp.asarray(np.tril(np.ones((REC_CHUNK, REC_CHUNK), np.float32)), BF16)
    lvl = jnp.asarray(_rec_level_map())
    shape3 = lambda a: a.reshape(B, S, D)
    z1 = _rec(shape3(q1), shape3(fl1), shape3(v1), shape3(gs1), rec_lb, rec_out_norm[0:1],
              tril_rec, lvl)
    out = _post(z1.reshape(T, D), h, p[1].reshape(T, PLE_DIM), rec_w_out[0].astype(BF16),
                norm_post[1:2], ple_w_gate[1].astype(BF16), ple_w_proj[1].astype(BF16))
    return out.reshape(B, S, D)
```

```python
import numpy as np
import jax
import jax.numpy as jnp
from jax import lax
from jax.experimental import pallas as pl
from jax.experimental.pallas import tpu as pltpu

F32 = jnp.float32
BF16 = jnp.bfloat16

D_MODEL = 1024
PLE_DIM = 256
ATT_HEADS = 16
ATT_HEAD_DIM = 64
REC_HEADS = 8
REC_DIM = 128
EPS = 1e-6
LOG2_E = 1.4426950408889634
SKIP_LOG2 = 128.0
NORM_MARGIN = 1.01
DOT_MARGIN = 2.0 ** -7

LANES = 128
HEAD_PAIRS = ATT_HEADS // 2
C_PIECES = 3
BIAS_LANES_PER_HEAD = 4

ROW_TILE = 512
ATT_BLOCK = 512
Q_TILES = 2
Q_GROUP = 16
COL_TILE = 256
M_INIT = -1e30
REC_CHUNK = 256
REC_HEADS_PER_STEP = 8
REC_PIECES = 2
REC_QUARTER = 64
REC_MILD_DECAY = 100.0
VMEM_LIMIT = 56 * 1024 * 1024

NT_DIMS = (((1,), (1,)), ((), ()))
TN_DIMS = (((0,), (0,)), ((), ()))


def _dot(a, b):
    return jnp.dot(a, b, preferred_element_type=F32)


def _dot_nt(a, b):
    return lax.dot_general(a, b, NT_DIMS, preferred_element_type=F32)


def _dot_tn(a, b):
    return lax.dot_general(a, b, TN_DIMS, preferred_element_type=F32)


def _split_bf16(x, pieces):
    out = []
    r = x
    for _ in range(pieces - 1):
        p = r.astype(BF16)
        out.append(p)
        r = r - p.astype(F32)
    out.append(r.astype(BF16))
    return out


def _rms_norm(x, gain):
    ms = jnp.mean(x * x, axis=-1, keepdims=True)
    return x * lax.rsqrt(ms + EPS) * gain


def _sigmoid(x):
    return 1.0 / (1.0 + jnp.exp(-x))


def _fox_proj_kernel(x_ref, gain_ref, wq_ref, wk_ref, wv_ref, wgf_ref, bf_ref,
                     tril_ref, sel_ref, qt_ref, k_ref, vt_ref, gs_ref, caug_ref, nblk_ref,
                     carry_ref, kmax_tab_ref, clast_tab_ref):
    tm = x_ref.shape[1]
    i = pl.program_id(1)

    @pl.when(i == 0)
    def _():
        carry_ref[...] = jnp.zeros_like(carry_ref)
        kmax_tab_ref[...] = jnp.zeros_like(kmax_tab_ref)
        clast_tab_ref[...] = jnp.zeros_like(clast_tab_ref)

    u = _rms_norm(x_ref[0], gain_ref[...]).astype(BF16)

    qb = (_dot(u, wq_ref[...]) * (ATT_HEAD_DIM ** -0.5 * LOG2_E)).astype(BF16)
    kb = _dot(u, wk_ref[...]).astype(BF16)
    qt = qb.T
    qt_ref[0, 0] = qt
    k_ref[0] = kb
    qtf, ktf = qt.astype(F32), kb.T.astype(F32)
    per_head = lambda t: jnp.sum(t.reshape(ATT_HEADS, ATT_HEAD_DIM, tm), axis=1)
    qn2, kn2, qk = per_head(qtf * qtf), per_head(ktf * ktf), per_head(qtf * ktf)
    vt_ref[0, 0] = _dot(u, wv_ref[...]).astype(BF16).T
    gf = _dot(u, wgf_ref[...])
    g = gf[:, :D_MODEL]
    gs_ref[0] = (g * _sigmoid(g)).astype(BF16)

    fl = gf[:, D_MODEL:] + bf_ref[...]
    log_f = jnp.minimum(fl, 0.0) - jnp.log1p(jnp.exp(-jnp.abs(fl)))
    tril = tril_ref[...]
    c = sum(_dot(tril, piece) for piece in _split_bf16(log_f, C_PIECES)) + carry_ref[...]
    carry_ref[...] = c[tm - 1:tm, :]
    c2 = c * LOG2_E
    caug = sum(_dot(piece, sel_ref[n]) for n, piece in enumerate(_split_bf16(-c2, C_PIECES)))
    caug_ref[0] = caug.astype(BF16)

    c2_t = c2.T[:ATT_HEADS]
    qmax = jnp.sqrt(jnp.max(qn2, axis=1, keepdims=True)) * NORM_MARGIN
    kmax = jnp.sqrt(jnp.max(kn2, axis=1, keepdims=True)) * NORM_MARGIN
    m_low = jnp.min(qk - c2_t, axis=1, keepdims=True) - DOT_MARGIN * qmax * kmax
    blk = lax.broadcasted_iota(jnp.int32, kmax_tab_ref.shape, 1)
    kmax_tab = jnp.where(blk == i, kmax, kmax_tab_ref[...])
    clast_tab = jnp.where(blk == i, c2_t[:, tm - 1:tm], clast_tab_ref[...])
    kmax_tab_ref[...] = kmax_tab
    clast_tab_ref[...] = clast_tab
    bound = qmax * kmax_tab - clast_tab - m_low
    first_kept = jnp.min(jnp.where((blk < i) & (bound >= -SKIP_LOG2), blk, i), axis=1, keepdims=True)
    nblk_ref[0, 0] = jnp.broadcast_to(i - first_kept + 1, nblk_ref.shape[2:])


def _fox_proj(x, gain, wq, wk, wv, wgf, bf, tril, sel):
    B, S, D = x.shape
    tm = ROW_TILE
    nb = S // tm
    const = lambda shape: pl.BlockSpec(shape, lambda b, i: (0,) * len(shape))
    tok = pl.BlockSpec((1, tm, D), lambda b, i: (b, i, 0))
    out_tok = jax.ShapeDtypeStruct((B, S, D), BF16)
    tiled_t = pl.BlockSpec((1, 1, D, tm), lambda b, i: (b, i, 0, 0))
    out_t = jax.ShapeDtypeStruct((B, nb, D, tm), BF16)
    return pl.pallas_call(
        _fox_proj_kernel,
        grid=(B, nb),
        in_specs=[tok, const((1, D)), const((D, D)), const((D, D)), const((D, D)), const((D, D + LANES)),
                  const((1, LANES)), const((tm, tm)), const((C_PIECES, LANES, LANES))],
        out_specs=[tiled_t, tok, tiled_t, tok, pl.BlockSpec((1, tm, LANES), lambda b, i: (b, i, 0)),
                   pl.BlockSpec((1, 1, ATT_HEADS, LANES), lambda b, i: (b, i, 0, 0))],
        out_shape=[out_t, out_tok, out_t, out_tok, jax.ShapeDtypeStruct((B, S, LANES), BF16),
                   jax.ShapeDtypeStruct((B, nb, ATT_HEADS, LANES), jnp.int32)],
        scratch_shapes=[pltpu.VMEM((1, LANES), F32), pltpu.VMEM((ATT_HEADS, LANES), F32),
                        pltpu.VMEM((ATT_HEADS, LANES), F32)],
        compiler_params=pltpu.CompilerParams(
            dimension_semantics=("arbitrary", "arbitrary"), vmem_limit_bytes=VMEM_LIMIT),
    )(x, gain, wq, wk, wv, wgf, bf, tril, sel)


def _fox_attn_kernel(nblk_ref, qt_ref, k_ref, caug_ref, vt_ref, gs_ref, z_ref,
                     s0_scr, s1_scr, s2_scr, s3_scr, mx0_scr, mx1_scr, mx2_scr, mx3_scr, m_scr, acc_scr):
    bk = ATT_BLOCK
    bq = Q_TILES * ATT_BLOCK
    b, pair, grp = pl.program_id(0), pl.program_id(1), pl.program_id(2)
    nb = pl.num_programs(2) * Q_GROUP
    ones_rows = jnp.ones((16, bk), BF16)
    bufs = ((s0_scr, mx0_scr), (s1_scr, mx1_scr), (s2_scr, mx2_scr), (s3_scr, mx3_scr))
    bias_row = lax.broadcasted_iota(jnp.int32, (LANES, bq), 0)
    no_head = jnp.zeros((ATT_HEAD_DIM, bq), BF16)

    def query_unit(sub, carry):
        unit = grp * (Q_GROUP // Q_TILES) + sub
        last = Q_TILES * unit + Q_TILES - 1
        q_t = jnp.concatenate([qt_ref[0, Q_TILES * sub + n] for n in range(Q_TILES)], axis=1)
        outs = []
        for hh in range(2):
            lo = hh * ATT_HEAD_DIM
            first_lane = BIAS_LANES_PER_HEAD * (2 * pair + hh)
            pick = jnp.where((bias_row >= first_lane) & (bias_row < first_lane + C_PIECES),
                             1.0, 0.0).astype(BF16)
            parts = [q_t[:ATT_HEAD_DIM], no_head] if hh == 0 else [no_head, q_t[ATT_HEAD_DIM:]]
            qt_aug = jnp.concatenate(parts + [pick], axis=0)

            def put_scores(j, dst, cols=slice(0, bq), diagonal=False):
                rows = pl.ds(pl.multiple_of(j * bk, bk), bk)
                k_aug = jnp.concatenate([k_ref[0, rows, :], caug_ref[0, rows, :]], axis=1)
                s_t = _dot(k_aug, qt_aug[:, cols])
                if diagonal:
                    shape = (bk, cols.stop - cols.start)
                    key_pos = j * bk + lax.broadcasted_iota(jnp.int32, shape, 0)
                    qry_pos = unit * bq + cols.start + lax.broadcasted_iota(jnp.int32, shape, 1)
                    s_t = jnp.where(key_pos <= qry_pos, s_t, -jnp.inf)
                dst[0][:, cols] = s_t
                dst[1][:, cols] = jnp.max(s_t, axis=0, keepdims=True)

            def update(j, src, cols=slice(0, bq)):
                m = m_scr[:, cols]
                m_new = jnp.maximum(m, src[1][:, cols])
                alpha = jnp.exp2(m - m_new)
                p_t = jnp.exp2((src[0][:, cols] - m_new).astype(BF16))
                v_aug = jnp.concatenate([vt_ref[0, j, lo:lo + ATT_HEAD_DIM, :], ones_rows], axis=0)
                acc_scr[:, cols] = alpha * acc_scr[:, cols] + _dot(v_aug, p_t)
                m_scr[:, cols] = m_new

            def run(*ops):
                split = [[slice(c, c + COL_TILE) for c in range(op[3].start, op[3].stop, COL_TILE)]
                         for op in ops]
                for n in range(max(len(tiles) for tiles in split)):
                    for (kind, j, buf, _, diagonal), tiles in zip(ops, split):
                        if n < len(tiles):
                            if kind == "put":
                                put_scores(j, buf, tiles[n], diagonal)
                            else:
                                update(j, buf, tiles[n])

            full = slice(0, bq)
            put = lambda j, buf, cols=full, diagonal=False: ("put", j, buf, cols, diagonal)
            upd = lambda j, buf, cols=full: ("upd", j, buf, cols, False)

            head_row = (b * ATT_HEADS + 2 * pair + hh) * nb
            n_steps = nblk_ref[head_row + last]
            for n in range(1, Q_TILES):
                n_steps = jnp.maximum(n_steps, nblk_ref[head_row + last - n] + n)
            m_scr[...] = jnp.full(m_scr.shape, M_INIT, F32)
            acc_scr[...] = jnp.zeros(acc_scr.shape, F32)
            last_tile = slice(bq - bk, bq)
            run(put(last, bufs[0], last_tile, True))
            run(put(last - 1, bufs[1], full, True))
            run(put(jnp.maximum(last - 2, 0), bufs[2]), upd(last, bufs[0], last_tile))

            def four_steps(g, c):
                t0 = 1 + 4 * g
                for n in range(4):
                    run(put(jnp.maximum(last - (t0 + n + 2), 0), bufs[(n + 3) % 4]),
                        upd(last - (t0 + n), bufs[(n + 1) % 4]))
                return c

            groups = (n_steps - 1) // 4
            lax.fori_loop(0, groups, four_steps, 0)
            t0 = 1 + 4 * groups
            rest = n_steps - t0

            @pl.when(rest == 1)
            def _():
                run(upd(last - t0, bufs[1]))

            @pl.when(rest == 2)
            def _():
                run(upd(last - t0, bufs[1]), upd(last - t0 - 1, bufs[2]))

            @pl.when(rest == 3)
            def _():
                run(put(last - t0 - 2, bufs[3]), upd(last - t0, bufs[1]))
                run(upd(last - t0 - 1, bufs[2]), upd(last - t0 - 2, bufs[3]))

            acc = acc_scr[...]
            outs.append(acc[:ATT_HEAD_DIM, :] / acc[ATT_HEAD_DIM:ATT_HEAD_DIM + 1, :])

        o = jnp.concatenate(outs, axis=0).T
        rows = pl.ds(pl.multiple_of(sub * bq, bq), bq)
        z_ref[0, rows, :] = (o * gs_ref[0, rows, :].astype(F32)).astype(BF16)
        return carry

    lax.fori_loop(0, Q_GROUP // Q_TILES, query_unit, 0)


def _fox_attn(nblk, qt, k, caug, vt, gs):
    B, S, D = k.shape
    bk = ATT_BLOCK
    bq = Q_TILES * ATT_BLOCK
    nb = S // bk
    tok = pl.BlockSpec((1, Q_GROUP * bk, LANES), lambda b, p, g, n: (b, g, p))
    seq = pl.BlockSpec((1, S, LANES), lambda b, p, g, n: (b, 0, p))
    return pl.pallas_call(
        _fox_attn_kernel,
        grid_spec=pltpu.PrefetchScalarGridSpec(
            num_scalar_prefetch=1,
            grid=(B, HEAD_PAIRS, nb // Q_GROUP),
            in_specs=[pl.BlockSpec((1, Q_GROUP, LANES, bk), lambda b, p, g, n: (b, g, p, 0)),
                      seq, pl.BlockSpec((1, S, LANES), lambda b, p, g, n: (b, 0, 0)),
                      pl.BlockSpec((1, nb, LANES, vt.shape[3]), lambda b, p, g, n: (b, 0, p, 0)),
                      tok],
            out_specs=tok,
            scratch_shapes=[pltpu.VMEM((bk, bq), F32)] * 4 + [pltpu.VMEM((1, bq), F32)] * 4
            + [pltpu.VMEM((1, bq), F32), pltpu.VMEM((ATT_HEAD_DIM + 16, bq), F32)]),
        out_shape=jax.ShapeDtypeStruct((B, S, D), BF16),
        compiler_params=pltpu.CompilerParams(
            dimension_semantics=("arbitrary", "arbitrary", "arbitrary"), vmem_limit_bytes=VMEM_LIMIT),
    )(nblk, qt, k, caug, vt, gs)


def _post_block(z_ref, h_ref, p_ref, wout_ref, gain_ref, wgate_ref, wproj_ref):
    y = _dot(z_ref[...], wout_ref[...])
    h1 = h_ref[...] + _rms_norm(y, gain_ref[...])
    gate = _sigmoid(_dot(h1.astype(BF16), wgate_ref[...]))
    return h1 + _dot(p_ref[...].astype(BF16), wproj_ref[...]) * gate


def _post_kernel(z_ref, h_ref, p_ref, wout_ref, gain_ref, wgate_ref, wproj_ref, o_ref):
    o_ref[...] = _post_block(z_ref, h_ref, p_ref, wout_ref, gain_ref, wgate_ref, wproj_ref)


def _post(z, h, p, wout, gain, wgate, wproj):
    T, D = h.shape
    tm = ROW_TILE
    const = lambda shape: pl.BlockSpec(shape, lambda i: (0,) * len(shape))
    tok = lambda w: pl.BlockSpec((tm, w), lambda i: (i, 0))
    return pl.pallas_call(
        _post_kernel,
        grid=(T // tm,),
        in_specs=[tok(D), tok(D), tok(PLE_DIM), const((D, D)), const((1, D)), const((D, D)),
                  const((PLE_DIM, D))],
        out_specs=tok(D),
        out_shape=jax.ShapeDtypeStruct((T, D), F32),
        compiler_params=pltpu.CompilerParams(
            dimension_semantics=("arbitrary",), vmem_limit_bytes=VMEM_LIMIT),
    )(z, h, p, wout, gain, wgate, wproj)


def _post_rec_proj_kernel(z_ref, h_ref, p_ref, wout_ref, gain_ref, wgate_ref, wproj_ref,
                          gain1_ref, wq_ref, wf_ref, wi_ref, wg_ref,
                          o_ref, q_ref, fl_ref, v_ref, gs_ref):
    h_new = _post_block(z_ref, h_ref, p_ref, wout_ref, gain_ref, wgate_ref, wproj_ref)
    o_ref[...] = h_new
    u = _rms_norm(h_new, gain1_ref[...]).astype(BF16)
    q_ref[...] = _dot(u, wq_ref[...]).astype(BF16)
    fl_ref[...] = _dot(u, wf_ref[...])
    v_ref[...] = _dot(u, wi_ref[...]).astype(BF16)
    g = _dot(u, wg_ref[...])
    gs_ref[...] = (g * _sigmoid(g)).astype(BF16)


def _post_rec_proj(z, h, p, wout, gain, wgate, wproj, gain1, wq, wf, wi, wg):
    T, D = h.shape
    tm = ROW_TILE
    const = lambda shape: pl.BlockSpec(shape, lambda i: (0,) * len(shape))
    tok = lambda w: pl.BlockSpec((tm, w), lambda i: (i, 0))
    bf = jax.ShapeDtypeStruct((T, D), BF16)
    f32 = jax.ShapeDtypeStruct((T, D), F32)
    return pl.pallas_call(
        _post_rec_proj_kernel,
        grid=(T // tm,),
        in_specs=[tok(D), tok(D), tok(PLE_DIM), const((D, D)), const((1, D)), const((D, D)),
                  const((PLE_DIM, D)), const((1, D)), const((D, D)), const((D, D)), const((D, D)),
                  const((D, D))],
        out_specs=[tok(D)] * 5,
        out_shape=[f32, bf, f32, bf, bf],
        compiler_params=pltpu.CompilerParams(
            dimension_semantics=("arbitrary",), vmem_limit_bytes=VMEM_LIMIT),
    )(z, h, p, wout, gain, wgate, wproj, gain1, wq, wf, wi, wg)


def _rec_levels():
    h = REC_CHUNK // 4
    out = []
    while h >= 1:
        out.append(h)
        h //= 2
    return out


def _rec_kernel(q_ref, fl_ref, v_ref, gs_ref, lb_ref, gain_ref, tril_ref, lvl_ref, z_ref,
                st_ref, b_scr, kb_scr):
    C = REC_CHUNK
    HALF = C // 2
    nq = C // REC_QUARTER

    @pl.when(pl.program_id(2) == 0)
    def _():
        st_ref[...] = jnp.zeros_like(st_ref)

    halves = (slice(0, HALF), slice(HALF, C))
    tril = tril_ref[...]

    def quarter_starts(b):
        ends = b.reshape(nq, REC_QUARTER, LANES)[:, REC_QUARTER - 1:, :]
        return jnp.concatenate([jnp.zeros_like(ends[:1]), ends[:-1]], axis=0), ends

    worst = jnp.zeros((1, LANES), F32)
    for hd in range(REC_HEADS_PER_STEP):
        lanes = slice(hd * REC_DIM, (hd + 1) * REC_DIM)
        lbs = lb_ref[:, lanes]
        e = jnp.exp(lbs - jnp.max(lbs, axis=0, keepdims=True))
        sm = e / jnp.sum(e, axis=0, keepdims=True)
        lb = (sm[0:1] + sm[1:2]) - sm[0:1]
        fl = fl_ref[0, :, lanes]
        log_f = jnp.log(lb + (1.0 - lb) * _sigmoid(fl))
        kb_scr[hd] = ((1.0 - lb) * _sigmoid(-fl)).astype(BF16)
        b = sum(_dot(tril, piece) for piece in _split_bf16(log_f, REC_PIECES)) * LOG2_E
        b_scr[hd] = b
        starts, ends = quarter_starts(b)
        worst = jnp.maximum(worst, jnp.max(starts - ends, axis=0))
    mild = jnp.max(worst) <= REC_MILD_DECAY

    def scores_and_output(hd, intra_quarter):
        lanes = slice(hd * REC_DIM, (hd + 1) * REC_DIM)
        b = b_scr[hd]
        qb = q_ref[0, :, lanes]
        kb = kb_scr[hd]
        v = v_ref[0, :, lanes]
        lvl = lvl_ref[...]

        def level_operands(ref):
            w = jnp.exp2(-jnp.abs(b - ref)).astype(BF16)
            return qb * w, kb * w

        def block_ref(h):
            b3 = b.reshape(C // (2 * h), 2 * h, LANES)
            return jnp.broadcast_to(b3[:, h - 1:h, :], b3.shape).reshape(C, LANES)

        levels = _rec_levels()
        if intra_quarter:
            starts, _ = quarter_starts(b)
            d = b - jnp.broadcast_to(starts, (nq, REC_QUARTER, LANES)).reshape(C, LANES)
            q_l, k_l = qb * jnp.exp2(d).astype(BF16), kb * jnp.exp2(-d).astype(BF16)
            inside = (lvl == 0) | (lvl > levels.index(REC_QUARTER) + 1)
            a_diag = [jnp.where(inside, _dot_nt(q_l[r], k_l[r]), 0.0) for r in halves]
            levels = [h for h in levels if h >= REC_QUARTER]
        else:
            a_diag = [jnp.where(lvl == 0, _dot_nt(qb[r], kb[r]), 0.0) for r in halves]
        row = lax.broadcasted_iota(jnp.int32, (C, LANES), 0)
        for h in levels:
            if h >= 4:
                ref = block_ref(h)
            elif h == 2:
                o4 = row & 3
                ref = jnp.where(o4 == 0, pltpu.roll(b, C - 1, 0),
                                jnp.where(o4 == 1, b, jnp.where(o4 == 2, pltpu.roll(b, 1, 0),
                                                                pltpu.roll(b, 2, 0))))
            else:
                ref = jnp.where((row & 1) == 0, b, pltpu.roll(b, 1, 0))
            q_l, k_l = level_operands(ref)
            own = lvl == _rec_levels().index(h) + 1
            a_diag = [jnp.where(own, _dot_nt(q_l[r], k_l[r]), a) for r, a in zip(halves, a_diag)]
        q_l, k_l = level_operands(b[HALF - 1:HALF, :])
        a_cross = _dot_nt(q_l[halves[1]], k_l[halves[0]])
        a = jnp.concatenate([jnp.concatenate([a_diag[0], jnp.zeros_like(a_cross)], axis=1),
                             jnp.concatenate([a_cross, a_diag[1]], axis=1)], axis=0)

        st = st_ref[hd]
        q_dec = qb * jnp.exp2(b).astype(BF16)
        o = _dot(a.astype(BF16), v) + _dot_nt(q_dec, st.astype(BF16))
        b_last = b[C - 1:C, :]
        k_dec = kb * jnp.exp2(b_last - b).astype(BF16)
        st_ref[hd] = st * jnp.exp2(b_last) + _dot_tn(v, k_dec)

        o = o * lax.rsqrt(jnp.mean(o * o, axis=-1, keepdims=True) + EPS) * gain_ref[...]
        z_ref[0, :, lanes] = (o * gs_ref[0, :, lanes].astype(F32)).astype(BF16)

    @pl.when(mild)
    def _():
        for hd in range(REC_HEADS_PER_STEP):
            scores_and_output(hd, intra_quarter=True)

    @pl.when(jnp.logical_not(mild))
    def _():
        for hd in range(REC_HEADS_PER_STEP):
            scores_and_output(hd, intra_quarter=False)


def _rec_level_map():
    size = REC_CHUNK // 2
    t = np.arange(size)[:, None]
    s = np.arange(size)[None, :]
    out = np.full((size, size), -1, np.int32)
    out[t == s] = 0
    for n, h in enumerate(_rec_levels()):
        own = (t // (2 * h) == s // (2 * h)) & ((t & h) != 0) & ((s & h) == 0)
        out[own] = n + 1
    return out


def _rec(q, fl, v, gs, rec_lb, gain, tril, lvl):
    B, S, D = q.shape
    C = REC_CHUNK
    width = REC_HEADS_PER_STEP * REC_DIM
    tok = pl.BlockSpec((1, C, width), lambda b, h, i: (b, i, h))
    const = lambda shape: pl.BlockSpec(shape, lambda b, h, i: (0,) * len(shape))
    return pl.pallas_call(
        _rec_kernel,
        grid=(B, REC_HEADS // REC_HEADS_PER_STEP, S // C),
        in_specs=[tok, tok, tok, tok,
                  pl.BlockSpec((rec_lb.shape[0], width), lambda b, h, i: (0, h)),
                  const((1, REC_DIM)), const((C, C)), const((C // 2, C // 2))],
        out_specs=tok,
        out_shape=jax.ShapeDtypeStruct((B, S, D), BF16),
        scratch_shapes=[pltpu.VMEM((REC_HEADS_PER_STEP, REC_DIM, REC_DIM), F32),
                        pltpu.VMEM((REC_HEADS_PER_STEP, C, REC_DIM), F32),
                        pltpu.VMEM((REC_HEADS_PER_STEP, C, REC_DIM), BF16)],
        compiler_params=pltpu.CompilerParams(
            dimension_semantics=("arbitrary", "arbitrary", "arbitrary"), vmem_limit_bytes=VMEM_LIMIT),
    )(q, fl, v, gs, rec_lb, gain, tril, lvl)


def _bias_lane(head, piece):
    return BIAS_LANES_PER_HEAD * head + piece


def _bias_selector():
    sel = np.zeros((C_PIECES, LANES, LANES), np.float32)
    for head in range(ATT_HEADS):
        for i in range(C_PIECES):
            sel[i, head, _bias_lane(head, i)] = 1.0
    return sel


def kernel(x, p, norm_pre, norm_post, att_w_in, att_b_f, att_w_out, rec_w_in, rec_lb, rec_out_norm,
           rec_w_out, ple_w_proj, ple_w_gate):
    B, S, D = x.shape
    assert D == D_MODEL and ROW_TILE == ATT_BLOCK and Q_TILES == 2
    assert S % (Q_GROUP * ATT_BLOCK) == 0 and S % REC_CHUNK == 0 and S // ATT_BLOCK <= LANES
    assert norm_pre.shape[0] == 2 and rec_lb.shape[0] == 2
    T = B * S
    W = ATT_HEADS * ATT_HEAD_DIM

    w_in = att_w_in[0]
    wq, wk, wv = (w_in[:, n * W:(n + 1) * W].astype(BF16) for n in range(3))
    wgf = jnp.pad(w_in[:, 3 * W:], ((0, 0), (0, LANES - ATT_HEADS))).astype(BF16)
    bf = jnp.pad(att_b_f[0], (0, LANES - ATT_HEADS)).reshape(1, LANES)
    tril_row = jnp.asarray(np.tril(np.ones((ROW_TILE, ROW_TILE), np.float32)), BF16)
    sel = jnp.asarray(_bias_selector(), BF16)
    qt, k, vt, gs, caug, nblk = _fox_proj(x, norm_pre[0:1], wq, wk, wv, wgf, bf, tril_row, sel)
    nblk = jnp.transpose(nblk[:, :, :, 0], (0, 2, 1)).reshape(-1)
    z = _fox_attn(nblk, qt, k, caug, vt, gs)
    r_in = rec_w_in[0]
    RW = REC_HEADS * REC_DIM
    rq, rf, ri, rg = (r_in[:, n * RW:(n + 1) * RW].astype(BF16) for n in range(4))
    h, q1, fl1, v1, gs1 = _post_rec_proj(
        z.reshape(T, D), x.reshape(T, D), p[0].reshape(T, PLE_DIM), att_w_out[0].astype(BF16),
        norm_post[0:1], ple_w_gate[0].astype(BF16), ple_w_proj[0].astype(BF16),
        norm_pre[1:2], rq, rf, ri, rg)
    tril_rec = jnp.asarray(np.tril(np.ones((REC_CHUNK, REC_CHUNK), np.float32)), BF16)
    lvl = jnp.asarray(_rec_level_map())
    shape3 = lambda a: a.reshape(B, S, D)
    z1 = _rec(shape3(q1), shape3(fl1), shape3(v1), shape3(gs1), rec_lb, rec_out_norm[0:1],
              tril_rec, lvl)
    out = _post(z1.reshape(T, D), h, p[1].reshape(T, PLE_DIM), rec_w_out[0].astype(BF16),
                norm_post[1:2], ple_w_gate[1].astype(BF16), ple_w_proj[1].astype(BF16))
    return out.reshape(B, S, D)
```

```python
import numpy as np
import jax
import jax.numpy as jnp
from jax import lax
from jax.experimental import pallas as pl
from jax.experimental.pallas import tpu as pltpu

F32 = jnp.float32
BF16 = jnp.bfloat16

D_MODEL = 1024
PLE_DIM = 256
ATT_HEADS = 16
ATT_HEAD_DIM = 64
REC_HEADS = 8
REC_DIM = 128
EPS = 1e-6
LOG2_E = 1.4426950408889634
SKIP_LOG2 = 128.0
NORM_MARGIN = 1.01
DOT_MARGIN = 2.0 ** -7

LANES = 128
HEAD_PAIRS = ATT_HEADS // 2
C_PIECES = 3
BIAS_LANES_PER_HEAD = 4

ROW_TILE = 512
ATT_BLOCK = 512
Q_TILES = 2
Q_GROUP = 16
COL_TILE = 256
M_INIT = -1e30
REC_CHUNK = 256
REC_HEADS_PER_STEP = 8
REC_PIECES = 2
REC_QUARTER = 64
REC_MILD_DECAY = 100.0
VMEM_LIMIT = 56 * 1024 * 1024

NT_DIMS = (((1,), (1,)), ((), ()))
TN_DIMS = (((0,), (0,)), ((), ()))


def _dot(a, b):
    return jnp.dot(a, b, preferred_element_type=F32)


def _dot_nt(a, b):
    return lax.dot_general(a, b, NT_DIMS, preferred_element_type=F32)


def _dot_tn(a, b):
    return lax.dot_general(a, b, TN_DIMS, preferred_element_type=F32)


def _split_bf16(x, pieces):
    out = []
    r = x
    for _ in range(pieces - 1):
        p = r.astype(BF16)
        out.append(p)
        r = r - p.astype(F32)
    out.append(r.astype(BF16))
    return out


def _rms_norm(x, gain):
    ms = jnp.mean(x * x, axis=-1, keepdims=True)
    return x * lax.rsqrt(ms + EPS) * gain


def _sigmoid(x):
    return 1.0 / (1.0 + jnp.exp(-x))


def _fox_proj_kernel(x_ref, gain_ref, wq_ref, wk_ref, wv_ref, wgf_ref, bf_ref,
                     tril_ref, sel_ref, qt_ref, k_ref, vt_ref, gs_ref, caug_ref, nblk_ref,
                     carry_ref, kmax_tab_ref, clast_tab_ref):
    tm = x_ref.shape[1]
    i = pl.program_id(1)

    @pl.when(i == 0)
    def _():
        carry_ref[...] = jnp.zeros_like(carry_ref)
        kmax_tab_ref[...] = jnp.zeros_like(kmax_tab_ref)
        clast_tab_ref[...] = jnp.zeros_like(clast_tab_ref)

    u = _rms_norm(x_ref[0], gain_ref[...]).astype(BF16)

    qb = (_dot(u, wq_ref[...]) * (ATT_HEAD_DIM ** -0.5 * LOG2_E)).astype(BF16)
    kb = _dot(u, wk_ref[...]).astype(BF16)
    qt = qb.T
    qt_ref[0, 0] = qt
    k_ref[0] = kb
    qtf, ktf = qt.astype(F32), kb.T.astype(F32)
    per_head = lambda t: jnp.sum(t.reshape(ATT_HEADS, ATT_HEAD_DIM, tm), axis=1)
    qn2, kn2, qk = per_head(qtf * qtf), per_head(ktf * ktf), per_head(qtf * ktf)
    vt_ref[0, 0] = _dot(u, wv_ref[...]).astype(BF16).T
    gf = _dot(u, wgf_ref[...])
    g = gf[:, :D_MODEL]
    gs_ref[0] = (g * _sigmoid(g)).astype(BF16)

    fl = gf[:, D_MODEL:] + bf_ref[...]
    log_f = jnp.minimum(fl, 0.0) - jnp.log1p(jnp.exp(-jnp.abs(fl)))
    tril = tril_ref[...]
    c = sum(_dot(tril, piece) for piece in _split_bf16(log_f, C_PIECES)) + carry_ref[...]
    carry_ref[...] = c[tm - 1:tm, :]
    c2 = c * LOG2_E
    caug = sum(_dot(piece, sel_ref[n]) for n, piece in enumerate(_split_bf16(-c2, C_PIECES)))
    caug_ref[0] = caug.astype(BF16)

    c2_t = c2.T[:ATT_HEADS]
    qmax = jnp.sqrt(jnp.max(qn2, axis=1, keepdims=True)) * NORM_MARGIN
    kmax = jnp.sqrt(jnp.max(kn2, axis=1, keepdims=True)) * NORM_MARGIN
    m_low = jnp.min(qk - c2_t, axis=1, keepdims=True) - DOT_MARGIN * qmax * kmax
    blk = lax.broadcasted_iota(jnp.int32, kmax_tab_ref.shape, 1)
    kmax_tab = jnp.where(blk == i, kmax, kmax_tab_ref[...])
    clast_tab = jnp.where(blk == i, c2_t[:, tm - 1:tm], clast_tab_ref[...])
    kmax_tab_ref[...] = kmax_tab
    clast_tab_ref[...] = clast_tab
    bound = qmax * kmax_tab - clast_tab - m_low
    first_kept = jnp.min(jnp.where((blk < i) & (bound >= -SKIP_LOG2), blk, i), axis=1, keepdims=True)
    nblk_ref[0, 0] = jnp.broadcast_to(i - first_kept + 1, nblk_ref.shape[2:])


def _fox_proj(x, gain, wq, wk, wv, wgf, bf, tril, sel):
    B, S, D = x.shape
    tm = ROW_TILE
    nb = S // tm
    const = lambda shape: pl.BlockSpec(shape, lambda b, i: (0,) * len(shape))
    tok = pl.BlockSpec((1, tm, D), lambda b, i: (b, i, 0))
    out_tok = jax.ShapeDtypeStruct((B, S, D), BF16)
    tiled_t = pl.BlockSpec((1, 1, D, tm), lambda b, i: (b, i, 0, 0))
    out_t = jax.ShapeDtypeStruct((B, nb, D, tm), BF16)
    return pl.pallas_call(
        _fox_proj_kernel,
        grid=(B, nb),
        in_specs=[tok, const((1, D)), const((D, D)), const((D, D)), const((D, D)), const((D, D + LANES)),
                  const((1, LANES)), const((tm, tm)), const((C_PIECES, LANES, LANES))],
        out_specs=[tiled_t, tok, tiled_t, tok, pl.BlockSpec((1, tm, LANES), lambda b, i: (b, i, 0)),
                   pl.BlockSpec((1, 1, ATT_HEADS, LANES), lambda b, i: (b, i, 0, 0))],
        out_shape=[out_t, out_tok, out_t, out_tok, jax.ShapeDtypeStruct((B, S, LANES), BF16),
                   jax.ShapeDtypeStruct((B, nb, ATT_HEADS, LANES), jnp.int32)],
        scratch_shapes=[pltpu.VMEM((1, LANES), F32), pltpu.VMEM((ATT_HEADS, LANES), F32),
                        pltpu.VMEM((ATT_HEADS, LANES), F32)],
        compiler_params=pltpu.CompilerParams(
            dimension_semantics=("arbitrary", "arbitrary"), vmem_limit_bytes=VMEM_LIMIT),
    )(x, gain, wq, wk, wv, wgf, bf, tril, sel)


def _fox_attn_kernel(nblk_ref, qt_ref, k_ref, caug_ref, vt_ref, gs_ref, z_ref,
                     s0_scr, s1_scr, s2_scr, s3_scr, mx0_scr, mx1_scr, mx2_scr, mx3_scr, m_scr, acc_scr):
    bk = ATT_BLOCK
    bq = Q_TILES * ATT_BLOCK
    b, pair, grp = pl.program_id(0), pl.program_id(1), pl.program_id(2)
    nb = pl.num_programs(2) * Q_GROUP
    ones_rows = jnp.ones((16, bk), BF16)
    bufs = ((s0_scr, mx0_scr), (s1_scr, mx1_scr), (s2_scr, mx2_scr), (s3_scr, mx3_scr))
    bias_row = lax.broadcasted_iota(jnp.int32, (LANES, bq), 0)
    no_head = jnp.zeros((ATT_HEAD_DIM, bq), BF16)

    def query_unit(sub, carry):
        unit = grp * (Q_GROUP // Q_TILES) + sub
        last = Q_TILES * unit + Q_TILES - 1
        q_t = jnp.concatenate([qt_ref[0, Q_TILES * sub + n] for n in range(Q_TILES)], axis=1)
        outs = []
        for hh in range(2):
            lo = hh * ATT_HEAD_DIM
            first_lane = BIAS_LANES_PER_HEAD * (2 * pair + hh)
            pick = jnp.where((bias_row >= first_lane) & (bias_row < first_lane + C_PIECES),
                             1.0, 0.0).astype(BF16)
            parts = [q_t[:ATT_HEAD_DIM], no_head] if hh == 0 else [no_head, q_t[ATT_HEAD_DIM:]]
            qt_aug = jnp.concatenate(parts + [pick], axis=0)

            def put_scores(j, dst, cols=slice(0, bq), diagonal=False):
                rows = pl.ds(pl.multiple_of(j * bk, bk), bk)
                k_aug = jnp.concatenate([k_ref[0, rows, :], caug_ref[0, rows, :]], axis=1)
                s_t = _dot(k_aug, qt_aug[:, cols])
                if diagonal:
                    shape = (bk, cols.stop - cols.start)
                    key_pos = j * bk + lax.broadcasted_iota(jnp.int32, shape, 0)
                    qry_pos = unit * bq + cols.start + lax.broadcasted_iota(jnp.int32, shape, 1)
                    s_t = jnp.where(key_pos <= qry_pos, s_t, -jnp.inf)
                dst[0][:, cols] = s_t
                dst[1][:, cols] = jnp.max(s_t, axis=0, keepdims=True)

            def update(j, src, cols=slice(0, bq)):
                m = m_scr[:, cols]
                m_new = jnp.maximum(m, src[1][:, cols])
                alpha = jnp.exp2(m - m_new)
                p_t = jnp.exp2((src[0][:, cols] - m_new).astype(BF16))
                v_aug = jnp.concatenate([vt_ref[0, j, lo:lo + ATT_HEAD_DIM, :], ones_rows], axis=0)
                acc_scr[:, cols] = alpha * acc_scr[:, cols] + _dot(v_aug, p_t)
                m_scr[:, cols] = m_new

            def run(*ops):
                split = [[slice(c, c + COL_TILE) for c in range(op[3].start, op[3].stop, COL_TILE)]
                         for op in ops]
                for n in range(max(len(tiles) for tiles in split)):
                    for (kind, j, buf, _, diagonal), tiles in zip(ops, split):
                        if n < len(tiles):
                            if kind == "put":
                                put_scores(j, buf, tiles[n], diagonal)
                            else:
                                update(j, buf, tiles[n])

            full = slice(0, bq)
            put = lambda j, buf, cols=full, diagonal=False: ("put", j, buf, cols, diagonal)
            upd = lambda j, buf, cols=full: ("upd", j, buf, cols, False)

            head_row = (b * ATT_HEADS + 2 * pair + hh) * nb
            n_steps = nblk_ref[head_row + last]
            for n in range(1, Q_TILES):
                n_steps = jnp.maximum(n_steps, nblk_ref[head_row + last - n] + n)
            m_scr[...] = jnp.full(m_scr.shape, M_INIT, F32)
            acc_scr[...] = jnp.zeros(acc_scr.shape, F32)
            last_tile = slice(bq - bk, bq)
            run(put(last, bufs[0], last_tile, True))
            run(put(last - 1, bufs[1], full, True))
            run(put(jnp.maximum(last - 2, 0), bufs[2]), upd(last, bufs[0], last_tile))

            def step(t, k, produce=True):
                ops = [put(last - (t + 2), bufs[(3 + k) % 4])] if produce else []
                return ops + [upd(last - t, bufs[(1 + k) % 4])]

            def four_steps(g, c):
                for n in range(4):
                    run(*step(1 + 4 * g + n, n))
                return c

            pipelined = jnp.maximum(n_steps - 3, 0)
            groups = pipelined // 4
            lax.fori_loop(0, groups, four_steps, 0)
            t0 = 1 + 4 * groups
            rest = pipelined - 4 * groups

            @pl.when(n_steps == 2)
            def _():
                run(upd(last - 1, bufs[1]))

            for r in range(4):
                @pl.when((n_steps >= 3) & (rest == r))
                def _(r=r):
                    for n in range(r):
                        run(*step(t0 + n, n))
                    run(*step(t0 + r, r, produce=False), *step(t0 + r + 1, r + 1, produce=False))

            acc = acc_scr[...]
            outs.append(acc[:ATT_HEAD_DIM, :] / acc[ATT_HEAD_DIM:ATT_HEAD_DIM + 1, :])

        o = jnp.concatenate(outs, axis=0).T
        rows = pl.ds(pl.multiple_of(sub * bq, bq), bq)
        z_ref[0, rows, :] = (o * gs_ref[0, rows, :].astype(F32)).astype(BF16)
        return carry

    lax.fori_loop(0, Q_GROUP // Q_TILES, query_unit, 0)


def _fox_attn(nblk, qt, k, caug, vt, gs):
    B, S, D = k.shape
    bk = ATT_BLOCK
    bq = Q_TILES * ATT_BLOCK
    nb = S // bk
    tok = pl.BlockSpec((1, Q_GROUP * bk, LANES), lambda b, p, g, n: (b, g, p))
    seq = pl.BlockSpec((1, S, LANES), lambda b, p, g, n: (b, 0, p))
    return pl.pallas_call(
        _fox_attn_kernel,
        grid_spec=pltpu.PrefetchScalarGridSpec(
            num_scalar_prefetch=1,
            grid=(B, HEAD_PAIRS, nb // Q_GROUP),
            in_specs=[pl.BlockSpec((1, Q_GROUP, LANES, bk), lambda b, p, g, n: (b, g, p, 0)),
                      seq, pl.BlockSpec((1, S, LANES), lambda b, p, g, n: (b, 0, 0)),
                      pl.BlockSpec((1, nb, LANES, vt.shape[3]), lambda b, p, g, n: (b, 0, p, 0)),
                      tok],
            out_specs=tok,
            scratch_shapes=[pltpu.VMEM((bk, bq), F32)] * 4 + [pltpu.VMEM((1, bq), F32)] * 4
            + [pltpu.VMEM((1, bq), F32), pltpu.VMEM((ATT_HEAD_DIM + 16, bq), F32)]),
        out_shape=jax.ShapeDtypeStruct((B, S, D), BF16),
        compiler_params=pltpu.CompilerParams(
            dimension_semantics=("arbitrary", "arbitrary", "arbitrary"), vmem_limit_bytes=VMEM_LIMIT),
    )(nblk, qt, k, caug, vt, gs)


def _post_block(z_ref, h_ref, p_ref, wout_ref, gain_ref, wgate_ref, wproj_ref):
    y = _dot(z_ref[...], wout_ref[...])
    h1 = h_ref[...] + _rms_norm(y, gain_ref[...])
    gate = _sigmoid(_dot(h1.astype(BF16), wgate_ref[...]))
    return h1 + _dot(p_ref[...].astype(BF16), wproj_ref[...]) * gate


def _post_kernel(z_ref, h_ref, p_ref, wout_ref, gain_ref, wgate_ref, wproj_ref, o_ref):
    o_ref[...] = _post_block(z_ref, h_ref, p_ref, wout_ref, gain_ref, wgate_ref, wproj_ref)


def _post(z, h, p, wout, gain, wgate, wproj):
    T, D = h.shape
    tm = ROW_TILE
    const = lambda shape: pl.BlockSpec(shape, lambda i: (0,) * len(shape))
    tok = lambda w: pl.BlockSpec((tm, w), lambda i: (i, 0))
    return pl.pallas_call(
        _post_kernel,
        grid=(T // tm,),
        in_specs=[tok(D), tok(D), tok(PLE_DIM), const((D, D)), const((1, D)), const((D, D)),
                  const((PLE_DIM, D))],
        out_specs=tok(D),
        out_shape=jax.ShapeDtypeStruct((T, D), F32),
        compiler_params=pltpu.CompilerParams(
            dimension_semantics=("arbitrary",), vmem_limit_bytes=VMEM_LIMIT),
    )(z, h, p, wout, gain, wgate, wproj)


def _post_rec_proj_kernel(z_ref, h_ref, p_ref, wout_ref, gain_ref, wgate_ref, wproj_ref,
                          gain1_ref, wq_ref, wf_ref, wi_ref, wg_ref,
                          o_ref, q_ref, fl_ref, v_ref, gs_ref):
    h_new = _post_block(z_ref, h_ref, p_ref, wout_ref, gain_ref, wgate_ref, wproj_ref)
    o_ref[...] = h_new
    u = _rms_norm(h_new, gain1_ref[...]).astype(BF16)
    q_ref[...] = _dot(u, wq_ref[...]).astype(BF16)
    fl_ref[...] = _dot(u, wf_ref[...])
    v_ref[...] = _dot(u, wi_ref[...]).astype(BF16)
    g = _dot(u, wg_ref[...])
    gs_ref[...] = (g * _sigmoid(g)).astype(BF16)


def _post_rec_proj(z, h, p, wout, gain, wgate, wproj, gain1, wq, wf, wi, wg):
    T, D = h.shape
    tm = ROW_TILE
    const = lambda shape: pl.BlockSpec(shape, lambda i: (0,) * len(shape))
    tok = lambda w: pl.BlockSpec((tm, w), lambda i: (i, 0))
    bf = jax.ShapeDtypeStruct((T, D), BF16)
    f32 = jax.ShapeDtypeStruct((T, D), F32)
    return pl.pallas_call(
        _post_rec_proj_kernel,
        grid=(T // tm,),
        in_specs=[tok(D), tok(D), tok(PLE_DIM), const((D, D)), const((1, D)), const((D, D)),
                  const((PLE_DIM, D)), const((1, D)), const((D, D)), const((D, D)), const((D, D)),
                  const((D, D))],
        out_specs=[tok(D)] * 5,
        out_shape=[f32, bf, f32, bf, bf],
        compiler_params=pltpu.CompilerParams(
            dimension_semantics=("arbitrary",), vmem_limit_bytes=VMEM_LIMIT),
    )(z, h, p, wout, gain, wgate, wproj, gain1, wq, wf, wi, wg)


def _rec_levels():
    h = REC_CHUNK // 4
    out = []
    while h >= 1:
        out.append(h)
        h //= 2
    return out


def _rec_kernel(q_ref, fl_ref, v_ref, gs_ref, lb_ref, gain_ref, tril_ref, lvl_ref, z_ref,
                st_ref, b_scr, kb_scr):
    C = REC_CHUNK
    HALF = C // 2
    nq = C // REC_QUARTER

    @pl.when(pl.program_id(2) == 0)
    def _():
        st_ref[...] = jnp.zeros_like(st_ref)

    halves = (slice(0, HALF), slice(HALF, C))
    tril = tril_ref[...]

    def quarter_starts(b):
        ends = b.reshape(nq, REC_QUARTER, LANES)[:, REC_QUARTER - 1:, :]
        return jnp.concatenate([jnp.zeros_like(ends[:1]), ends[:-1]], axis=0), ends

    worst = jnp.zeros((1, LANES), F32)
    for hd in range(REC_HEADS_PER_STEP):
        lanes = slice(hd * REC_DIM, (hd + 1) * REC_DIM)
        lbs = lb_ref[:, lanes]
        e = jnp.exp(lbs - jnp.max(lbs, axis=0, keepdims=True))
        sm = e / jnp.sum(e, axis=0, keepdims=True)
        lb = (sm[0:1] + sm[1:2]) - sm[0:1]
        fl = fl_ref[0, :, lanes]
        log_f = jnp.log(lb + (1.0 - lb) * _sigmoid(fl))
        kb_scr[hd] = ((1.0 - lb) * _sigmoid(-fl)).astype(BF16)
        b = sum(_dot(tril, piece) for piece in _split_bf16(log_f, REC_PIECES)) * LOG2_E
        b_scr[hd] = b
        starts, ends = quarter_starts(b)
        worst = jnp.maximum(worst, jnp.max(starts - ends, axis=0))
    mild = jnp.max(worst) <= REC_MILD_DECAY

    def scores_and_output(hd, intra_quarter):
        lanes = slice(hd * REC_DIM, (hd + 1) * REC_DIM)
        b = b_scr[hd]
        qb = q_ref[0, :, lanes]
        kb = kb_scr[hd]
        v = v_ref[0, :, lanes]
        lvl = lvl_ref[...]

        def level_operands(ref):
            w = jnp.exp2(-jnp.abs(b - ref)).astype(BF16)
            return qb * w, kb * w

        def block_ref(h):
            b3 = b.reshape(C // (2 * h), 2 * h, LANES)
            return jnp.broadcast_to(b3[:, h - 1:h, :], b3.shape).reshape(C, LANES)

        levels = _rec_levels()
        if intra_quarter:
            starts, _ = quarter_starts(b)
            d = b - jnp.broadcast_to(starts, (nq, REC_QUARTER, LANES)).reshape(C, LANES)
            q_l, k_l = qb * jnp.exp2(d).astype(BF16), kb * jnp.exp2(-d).astype(BF16)
            inside = (lvl == 0) | (lvl > levels.index(REC_QUARTER) + 1)
            a_diag = [jnp.where(inside, _dot_nt(q_l[r], k_l[r]), 0.0) for r in halves]
            levels = [h for h in levels if h >= REC_QUARTER]
        else:
            a_diag = [jnp.where(lvl == 0, _dot_nt(qb[r], kb[r]), 0.0) for r in halves]
        row = lax.broadcasted_iota(jnp.int32, (C, LANES), 0)
        for h in levels:
            if h >= 4:
                ref = block_ref(h)
            elif h == 2:
                o4 = row & 3
                ref = jnp.where(o4 == 0, pltpu.roll(b, C - 1, 0),
                                jnp.where(o4 == 1, b, jnp.where(o4 == 2, pltpu.roll(b, 1, 0),
                                                                pltpu.roll(b, 2, 0))))
            else:
                ref = jnp.where((row & 1) == 0, b, pltpu.roll(b, 1, 0))
            q_l, k_l = level_operands(ref)
            own = lvl == _rec_levels().index(h) + 1
            a_diag = [jnp.where(own, _dot_nt(q_l[r], k_l[r]), a) for r, a in zip(halves, a_diag)]
        q_l, k_l = level_operands(b[HALF - 1:HALF, :])
        a_cross = _dot_nt(q_l[halves[1]], k_l[halves[0]])
        a = jnp.concatenate([jnp.concatenate([a_diag[0], jnp.zeros_like(a_cross)], axis=1),
                             jnp.concatenate([a_cross, a_diag[1]], axis=1)], axis=0)

        st = st_ref[hd]
        q_dec = qb * jnp.exp2(b).astype(BF16)
        o = _dot(a.astype(BF16), v) + _dot_nt(q_dec, st.astype(BF16))
        b_last = b[C - 1:C, :]
        k_dec = kb * jnp.exp2(b_last - b).astype(BF16)
        st_ref[hd] = st * jnp.exp2(b_last) + _dot_tn(v, k_dec)

        o = o * lax.rsqrt(jnp.mean(o * o, axis=-1, keepdims=True) + EPS) * gain_ref[...]
        z_ref[0, :, lanes] = (o * gs_ref[0, :, lanes].astype(F32)).astype(BF16)

    @pl.when(mild)
    def _():
        for hd in range(REC_HEADS_PER_STEP):
            scores_and_output(hd, intra_quarter=True)

    @pl.when(jnp.logical_not(mild))
    def _():
        for hd in range(REC_HEADS_PER_STEP):
            scores_and_output(hd, intra_quarter=False)


def _rec_level_map():
    size = REC_CHUNK // 2
    t = np.arange(size)[:, None]
    s = np.arange(size)[None, :]
    out = np.full((size, size), -1, np.int32)
    out[t == s] = 0
    for n, h in enumerate(_rec_levels()):
        own = (t // (2 * h) == s // (2 * h)) & ((t & h) != 0) & ((s & h) == 0)
        out[own] = n + 1
    return out


def _rec(q, fl, v, gs, rec_lb, gain, tril, lvl):
    B, S, D = q.shape
    C = REC_CHUNK
    width = REC_HEADS_PER_STEP * REC_DIM
    tok = pl.BlockSpec((1, C, width), lambda b, h, i: (b, i, h))
    const = lambda shape: pl.BlockSpec(shape, lambda b, h, i: (0,) * len(shape))
    return pl.pallas_call(
        _rec_kernel,
        grid=(B, REC_HEADS // REC_HEADS_PER_STEP, S // C),
        in_specs=[tok, tok, tok, tok,
                  pl.BlockSpec((rec_lb.shape[0], width), lambda b, h, i: (0, h)),
                  const((1, REC_DIM)), const((C, C)), const((C // 2, C // 2))],
        out_specs=tok,
        out_shape=jax.ShapeDtypeStruct((B, S, D), BF16),
        scratch_shapes=[pltpu.VMEM((REC_HEADS_PER_STEP, REC_DIM, REC_DIM), F32),
                        pltpu.VMEM((REC_HEADS_PER_STEP, C, REC_DIM), F32),
                        pltpu.VMEM((REC_HEADS_PER_STEP, C, REC_DIM), BF16)],
        compiler_params=pltpu.CompilerParams(
            dimension_semantics=("arbitrary", "arbitrary", "arbitrary"), vmem_limit_bytes=VMEM_LIMIT),
    )(q, fl, v, gs, rec_lb, gain, tril, lvl)


def _bias_lane(head, piece):
    return BIAS_LANES_PER_HEAD * head + piece


def _bias_selector():
    sel = np.zeros((C_PIECES, LANES, LANES), np.float32)
    for head in range(ATT_HEADS):
        for i in range(C_PIECES):
            sel[i, head, _bias_lane(head, i)] = 1.0
    return sel


def kernel(x, p, norm_pre, norm_post, att_w_in, att_b_f, att_w_out, rec_w_in, rec_lb, rec_out_norm,
           rec_w_out, ple_w_proj, ple_w_gate):
    B, S, D = x.shape
    assert D == D_MODEL and ROW_TILE == ATT_BLOCK and Q_TILES == 2
    assert S % (Q_GROUP * ATT_BLOCK) == 0 and S % REC_CHUNK == 0 and S // ATT_BLOCK <= LANES
    assert norm_pre.shape[0] == 2 and rec_lb.shape[0] == 2
    T = B * S
    W = ATT_HEADS * ATT_HEAD_DIM

    w_in = att_w_in[0]
    wq, wk, wv = (w_in[:, n * W:(n + 1) * W].astype(BF16) for n in range(3))
    wgf = jnp.pad(w_in[:, 3 * W:], ((0, 0), (0, LANES - ATT_HEADS))).astype(BF16)
    bf = jnp.pad(att_b_f[0], (0, LANES - ATT_HEADS)).reshape(1, LANES)
    tril_row = jnp.asarray(np.tril(np.ones((ROW_TILE, ROW_TILE), np.float32)), BF16)
    sel = jnp.asarray(_bias_selector(), BF16)
    qt, k, vt, gs, caug, nblk = _fox_proj(x, norm_pre[0:1], wq, wk, wv, wgf, bf, tril_row, sel)
    nblk = jnp.transpose(nblk[:, :, :, 0], (0, 2, 1)).reshape(-1)
    z = _fox_attn(nblk, qt, k, caug, vt, gs)
    r_in = rec_w_in[0]
    RW = REC_HEADS * REC_DIM
    rq, rf, ri, rg = (r_in[:, n * RW:(n + 1) * RW].astype(BF16) for n in range(4))
    h, q1, fl1, v1, gs1 = _post_rec_proj(
        z.reshape(T, D), x.reshape(T, D), p[0].reshape(T, PLE_DIM), att_w_out[0].astype(BF16),
        norm_post[0:1], ple_w_gate[0].astype(BF16), ple_w_proj[0].astype(BF16),
        norm_pre[1:2], rq, rf, ri, rg)
    tril_rec = jnp.asarray(np.tril(np.ones((REC_CHUNK, REC_CHUNK), np.float32)), BF16)
    lvl = jnp.asarray(_rec_level_map())
    shape3 = lambda a: a.reshape(B, S, D)
    z1 = _rec(shape3(q1), shape3(fl1), shape3(v1), shape3(gs1), rec_lb, rec_out_norm[0:1],
              tril_rec, lvl)
    out = _post(z1.reshape(T, D), h, p[1].reshape(T, PLE_DIM), rec_w_out[0].astype(BF16),
                norm_post[1:2], ple_w_gate[1].astype(BF16), ple_w_proj[1].astype(BF16))
    return out.reshape(B, S, D)
```

```python
import numpy as np
import jax
import jax.numpy as jnp
from jax import lax
from jax.experimental import pallas as pl
from jax.experimental.pallas import tpu as pltpu

F32 = jnp.float32
BF16 = jnp.bfloat16

D_MODEL = 1024
PLE_DIM = 256
ATT_HEADS = 16
ATT_HEAD_DIM = 64
REC_HEADS = 8
REC_DIM = 128
EPS = 1e-6
LOG2_E = 1.4426950408889634
SKIP_LOG2 = 128.0
NORM_MARGIN = 1.01
DOT_MARGIN = 2.0 ** -7

LANES = 128
HEAD_PAIRS = ATT_HEADS // 2
C_PIECES = 3
BIAS_LANES_PER_HEAD = 4

ROW_TILE = 512
ATT_BLOCK = 512
Q_TILES = 2
Q_GROUP = 8
COL_TILE = 256
M_INIT = -1e30
REC_CHUNK = 256
REC_HEADS_PER_STEP = 8
REC_PIECES = 2
REC_QUARTER = 64
REC_MILD_DECAY = 100.0
VMEM_LIMIT = 56 * 1024 * 1024

NT_DIMS = (((1,), (1,)), ((), ()))
TN_DIMS = (((0,), (0,)), ((), ()))


def _dot(a, b):
    return jnp.dot(a, b, preferred_element_type=F32)


def _dot_nt(a, b):
    return lax.dot_general(a, b, NT_DIMS, preferred_element_type=F32)


def _dot_tn(a, b):
    return lax.dot_general(a, b, TN_DIMS, preferred_element_type=F32)


def _split_bf16(x, pieces):
    out = []
    r = x
    for _ in range(pieces - 1):
        p = r.astype(BF16)
        out.append(p)
        r = r - p.astype(F32)
    out.append(r.astype(BF16))
    return out


def _rms_norm(x, gain):
    ms = jnp.mean(x * x, axis=-1, keepdims=True)
    return x * lax.rsqrt(ms + EPS) * gain


def _sigmoid(x):
    return 1.0 / (1.0 + jnp.exp(-x))


def _fox_proj_kernel(x_ref, gain_ref, wq_ref, wk_ref, wv_ref, wgf_ref, bf_ref,
                     tril_ref, sel_ref, qt_ref, k_ref, vt_ref, gs_ref, caug_ref, nblk_ref,
                     carry_ref, kmax_tab_ref, clast_tab_ref):
    tm = x_ref.shape[1]
    i = pl.program_id(1)

    @pl.when(i == 0)
    def _():
        carry_ref[...] = jnp.zeros_like(carry_ref)
        kmax_tab_ref[...] = jnp.zeros_like(kmax_tab_ref)
        clast_tab_ref[...] = jnp.zeros_like(clast_tab_ref)

    u = _rms_norm(x_ref[0], gain_ref[...]).astype(BF16)

    qb = (_dot(u, wq_ref[...]) * (ATT_HEAD_DIM ** -0.5 * LOG2_E)).astype(BF16)
    kb = _dot(u, wk_ref[...]).astype(BF16)
    qt = qb.T
    qt_ref[0, 0] = qt
    k_ref[0] = kb
    qtf, ktf = qt.astype(F32), kb.T.astype(F32)
    per_head = lambda t: jnp.sum(t.reshape(ATT_HEADS, ATT_HEAD_DIM, tm), axis=1)
    qn2, kn2, qk = per_head(qtf * qtf), per_head(ktf * ktf), per_head(qtf * ktf)
    vt_ref[0, 0] = _dot(u, wv_ref[...]).astype(BF16).T
    gf = _dot(u, wgf_ref[...])
    g = gf[:, :D_MODEL]
    gs_ref[0] = (g * _sigmoid(g)).astype(BF16)

    fl = gf[:, D_MODEL:] + bf_ref[...]
    log_f = jnp.minimum(fl, 0.0) - jnp.log1p(jnp.exp(-jnp.abs(fl)))
    tril = tril_ref[...]
    c = sum(_dot(tril, piece) for piece in _split_bf16(log_f, C_PIECES)) + carry_ref[...]
    carry_ref[...] = c[tm - 1:tm, :]
    c2 = c * LOG2_E
    caug = sum(_dot(piece, sel_ref[n]) for n, piece in enumerate(_split_bf16(-c2, C_PIECES)))
    caug_ref[0] = caug.astype(BF16)

    c2_t = c2.T[:ATT_HEADS]
    qmax = jnp.sqrt(jnp.max(qn2, axis=1, keepdims=True)) * NORM_MARGIN
    kmax = jnp.sqrt(jnp.max(kn2, axis=1, keepdims=True)) * NORM_MARGIN
    m_low = jnp.min(qk - c2_t, axis=1, keepdims=True) - DOT_MARGIN * qmax * kmax
    blk = lax.broadcasted_iota(jnp.int32, kmax_tab_ref.shape, 1)
    kmax_tab = jnp.where(blk == i, kmax, kmax_tab_ref[...])
    clast_tab = jnp.where(blk == i, c2_t[:, tm - 1:tm], clast_tab_ref[...])
    kmax_tab_ref[...] = kmax_tab
    clast_tab_ref[...] = clast_tab
    bound = qmax * kmax_tab - clast_tab - m_low
    first_kept = jnp.min(jnp.where((blk < i) & (bound >= -SKIP_LOG2), blk, i), axis=1, keepdims=True)
    nblk_ref[0, 0] = jnp.broadcast_to(i - first_kept + 1, nblk_ref.shape[2:])


def _fox_proj(x, gain, wq, wk, wv, wgf, bf, tril, sel):
    B, S, D = x.shape
    tm = ROW_TILE
    nb = S // tm
    const = lambda shape: pl.BlockSpec(shape, lambda b, i: (0,) * len(shape))
    tok = pl.BlockSpec((1, tm, D), lambda b, i: (b, i, 0))
    out_tok = jax.ShapeDtypeStruct((B, S, D), BF16)
    tiled_t = pl.BlockSpec((1, 1, D, tm), lambda b, i: (b, i, 0, 0))
    out_t = jax.ShapeDtypeStruct((B, nb, D, tm), BF16)
    return pl.pallas_call(
        _fox_proj_kernel,
        grid=(B, nb),
        in_specs=[tok, const((1, D)), const((D, D)), const((D, D)), const((D, D)), const((D, D + LANES)),
                  const((1, LANES)), const((tm, tm)), const((C_PIECES, LANES, LANES))],
        out_specs=[tiled_t, tok, tiled_t, tok, pl.BlockSpec((1, tm, LANES), lambda b, i: (b, i, 0)),
                   pl.BlockSpec((1, 1, ATT_HEADS, LANES), lambda b, i: (b, i, 0, 0))],
        out_shape=[out_t, out_tok, out_t, out_tok, jax.ShapeDtypeStruct((B, S, LANES), BF16),
                   jax.ShapeDtypeStruct((B, nb, ATT_HEADS, LANES), jnp.int32)],
        scratch_shapes=[pltpu.VMEM((1, LANES), F32), pltpu.VMEM((ATT_HEADS, LANES), F32),
                        pltpu.VMEM((ATT_HEADS, LANES), F32)],
        compiler_params=pltpu.CompilerParams(
            dimension_semantics=("arbitrary", "arbitrary"), vmem_limit_bytes=VMEM_LIMIT),
    )(x, gain, wq, wk, wv, wgf, bf, tril, sel)


def _fox_attn_kernel(nblk_ref, qt_ref, k_ref, caug_ref, vt_ref, gs_ref, z_ref, s_scr, mx_scr, m_scr, acc_scr):
    bk = ATT_BLOCK
    bq = Q_TILES * ATT_BLOCK
    b, pair, grp = pl.program_id(0), pl.program_id(1), pl.program_id(2)
    nb = pl.num_programs(2) * Q_GROUP
    ones_rows = jnp.ones((16, bk), BF16)
    bias_row = lax.broadcasted_iota(jnp.int32, (LANES, bq), 0)
    no_head = jnp.zeros((ATT_HEAD_DIM, bq), BF16)
    full = slice(0, bq)
    last_tile = slice(bq - bk, bq)

    def query_unit(sub, carry):
        unit = grp * (Q_GROUP // Q_TILES) + sub
        last = Q_TILES * unit + Q_TILES - 1
        q_t = jnp.concatenate([qt_ref[0, Q_TILES * sub + n] for n in range(Q_TILES)], axis=1)

        qt_aug, n_steps = [], []
        for hh in range(2):
            first_lane = BIAS_LANES_PER_HEAD * (2 * pair + hh)
            pick = jnp.where((bias_row >= first_lane) & (bias_row < first_lane + C_PIECES),
                             1.0, 0.0).astype(BF16)
            parts = [q_t[:ATT_HEAD_DIM], no_head] if hh == 0 else [no_head, q_t[ATT_HEAD_DIM:]]
            qt_aug.append(jnp.concatenate(parts + [pick], axis=0))
            head_row = (b * ATT_HEADS + 2 * pair + hh) * nb
            n = nblk_ref[head_row + last]
            for back in range(1, Q_TILES):
                n = jnp.maximum(n, nblk_ref[head_row + last - back] + back)
            n_steps.append(n)

        def put_scores(hh, j, buf, cols, diagonal):
            rows = pl.ds(pl.multiple_of(j * bk, bk), bk)
            k_aug = jnp.concatenate([k_ref[0, rows, :], caug_ref[0, rows, :]], axis=1)
            s_t = _dot(k_aug, qt_aug[hh][:, cols])
            if diagonal:
                shape = (bk, cols.stop - cols.start)
                key_pos = j * bk + lax.broadcasted_iota(jnp.int32, shape, 0)
                qry_pos = unit * bq + cols.start + lax.broadcasted_iota(jnp.int32, shape, 1)
                s_t = jnp.where(key_pos <= qry_pos, s_t, -jnp.inf)
            s_scr[hh, buf, :, cols] = s_t
            mx_scr[hh, buf, :, cols] = jnp.max(s_t, axis=0, keepdims=True)

        def update(hh, j, buf, cols):
            lo = hh * ATT_HEAD_DIM
            m = m_scr[hh, :, cols]
            m_new = jnp.maximum(m, mx_scr[hh, buf, :, cols])
            alpha = jnp.exp2(m - m_new)
            p_t = jnp.exp2((s_scr[hh, buf, :, cols] - m_new).astype(BF16))
            v_aug = jnp.concatenate([vt_ref[0, j, lo:lo + ATT_HEAD_DIM, :], ones_rows], axis=0)
            acc_scr[hh, :, cols] = alpha * acc_scr[hh, :, cols] + _dot(v_aug, p_t)
            m_scr[hh, :, cols] = m_new

        def run(*ops):
            split = [[slice(c, c + COL_TILE) for c in range(op[4].start, op[4].stop, COL_TILE)]
                     for op in ops]
            for n in range(max(len(tiles) for tiles in split)):
                for (kind, hh, j, buf, _, diagonal), tiles in zip(ops, split):
                    if n < len(tiles):
                        if kind == "put":
                            put_scores(hh, j, buf, tiles[n], diagonal)
                        else:
                            update(hh, j, buf, tiles[n])

        put = lambda hh, j, buf, cols=full, diagonal=False: ("put", hh, j, buf, cols, diagonal)
        upd = lambda hh, j, buf, cols=full: ("upd", hh, j, buf, cols, False)

        def step(hh, t, k, produce=True):
            ops = [put(hh, last - (t + 2), (3 + k) % 4)] if produce else []
            return ops + [upd(hh, last - t, (1 + k) % 4)]

        m_scr[...] = jnp.full(m_scr.shape, M_INIT, F32)
        acc_scr[...] = jnp.zeros(acc_scr.shape, F32)
        run(put(0, last, 0, last_tile, True), put(1, last, 0, last_tile, True))
        run(put(0, last - 1, 1, full, True), put(1, last - 1, 1, full, True))
        run(put(0, jnp.maximum(last - 2, 0), 2), upd(0, last, 0, last_tile),
            put(1, jnp.maximum(last - 2, 0), 2), upd(1, last, 0, last_tile))

        groups = [jnp.maximum(n - 3, 0) // 4 for n in n_steps]
        both = jnp.minimum(groups[0], groups[1])

        def four_steps_both(g, c):
            for n in range(4):
                run(*step(0, 1 + 4 * g + n, n), *step(1, 1 + 4 * g + n, n))
            return c

        lax.fori_loop(0, both, four_steps_both, 0)

        outs = []
        for hh in range(2):
            def four_steps(g, c, hh=hh):
                for n in range(4):
                    run(*step(hh, 1 + 4 * g + n, n))
                return c

            lax.fori_loop(both, groups[hh], four_steps, 0)
            t0 = 1 + 4 * groups[hh]
            rest = jnp.maximum(n_steps[hh] - 3, 0) - 4 * groups[hh]

            @pl.when(n_steps[hh] == 2)
            def _(hh=hh):
                run(upd(hh, last - 1, 1))

            for r in range(4):
                @pl.when((n_steps[hh] >= 3) & (rest == r))
                def _(r=r, hh=hh, t0=t0):
                    for n in range(r):
                        run(*step(hh, t0 + n, n))
                    run(*step(hh, t0 + r, r, produce=False), *step(hh, t0 + r + 1, r + 1, produce=False))

            acc = acc_scr[hh]
            outs.append(acc[:ATT_HEAD_DIM, :] / acc[ATT_HEAD_DIM:ATT_HEAD_DIM + 1, :])

        o = jnp.concatenate(outs, axis=0).T
        rows = pl.ds(pl.multiple_of(sub * bq, bq), bq)
        z_ref[0, rows, :] = (o * gs_ref[0, rows, :].astype(F32)).astype(BF16)
        return carry

    lax.fori_loop(0, Q_GROUP // Q_TILES, query_unit, 0)


def _fox_attn(nblk, qt, k, caug, vt, gs):
    B, S, D = k.shape
    bk = ATT_BLOCK
    bq = Q_TILES * ATT_BLOCK
    nb = S // bk
    tok = pl.BlockSpec((1, Q_GROUP * bk, LANES), lambda b, p, g, n: (b, g, p))
    seq = pl.BlockSpec((1, S, LANES), lambda b, p, g, n: (b, 0, p))
    return pl.pallas_call(
        _fox_attn_kernel,
        grid_spec=pltpu.PrefetchScalarGridSpec(
            num_scalar_prefetch=1,
            grid=(B, HEAD_PAIRS, nb // Q_GROUP),
            in_specs=[pl.BlockSpec((1, Q_GROUP, LANES, bk), lambda b, p, g, n: (b, g, p, 0)),
                      seq, pl.BlockSpec((1, S, LANES), lambda b, p, g, n: (b, 0, 0)),
                      pl.BlockSpec((1, nb, LANES, vt.shape[3]), lambda b, p, g, n: (b, 0, p, 0)),
                      tok],
            out_specs=tok,
            scratch_shapes=[pltpu.VMEM((2, 4, bk, bq), F32), pltpu.VMEM((2, 4, 1, bq), F32),
                            pltpu.VMEM((2, 1, bq), F32), pltpu.VMEM((2, ATT_HEAD_DIM + 16, bq), F32)]),
        out_shape=jax.ShapeDtypeStruct((B, S, D), BF16),
        compiler_params=pltpu.CompilerParams(
            dimension_semantics=("arbitrary", "arbitrary", "arbitrary"), vmem_limit_bytes=VMEM_LIMIT),
    )(nblk, qt, k, caug, vt, gs)


def _post_block(z_ref, h_ref, p_ref, wout_ref, gain_ref, wgate_ref, wproj_ref):
    y = _dot(z_ref[...], wout_ref[...])
    h1 = h_ref[...] + _rms_norm(y, gain_ref[...])
    gate = _sigmoid(_dot(h1.astype(BF16), wgate_ref[...]))
    return h1 + _dot(p_ref[...].astype(BF16), wproj_ref[...]) * gate


def _post_kernel(z_ref, h_ref, p_ref, wout_ref, gain_ref, wgate_ref, wproj_ref, o_ref):
    o_ref[...] = _post_block(z_ref, h_ref, p_ref, wout_ref, gain_ref, wgate_ref, wproj_ref)


def _post(z, h, p, wout, gain, wgate, wproj):
    T, D = h.shape
    tm = ROW_TILE
    const = lambda shape: pl.BlockSpec(shape, lambda i: (0,) * len(shape))
    tok = lambda w: pl.BlockSpec((tm, w), lambda i: (i, 0))
    return pl.pallas_call(
        _post_kernel,
        grid=(T // tm,),
        in_specs=[tok(D), tok(D), tok(PLE_DIM), const((D, D)), const((1, D)), const((D, D)),
                  const((PLE_DIM, D))],
        out_specs=tok(D),
        out_shape=jax.ShapeDtypeStruct((T, D), F32),
        compiler_params=pltpu.CompilerParams(
            dimension_semantics=("arbitrary",), vmem_limit_bytes=VMEM_LIMIT),
    )(z, h, p, wout, gain, wgate, wproj)


def _post_rec_proj_kernel(z_ref, h_ref, p_ref, wout_ref, gain_ref, wgate_ref, wproj_ref,
                          gain1_ref, wq_ref, wf_ref, wi_ref, wg_ref,
                          o_ref, q_ref, fl_ref, v_ref, gs_ref):
    h_new = _post_block(z_ref, h_ref, p_ref, wout_ref, gain_ref, wgate_ref, wproj_ref)
    o_ref[...] = h_new
    u = _rms_norm(h_new, gain1_ref[...]).astype(BF16)
    q_ref[...] = _dot(u, wq_ref[...]).astype(BF16)
    fl_ref[...] = _dot(u, wf_ref[...])
    v_ref[...] = _dot(u, wi_ref[...]).astype(BF16)
    g = _dot(u, wg_ref[...])
    gs_ref[...] = (g * _sigmoid(g)).astype(BF16)


def _post_rec_proj(z, h, p, wout, gain, wgate, wproj, gain1, wq, wf, wi, wg):
    T, D = h.shape
    tm = ROW_TILE
    const = lambda shape: pl.BlockSpec(shape, lambda i: (0,) * len(shape))
    tok = lambda w: pl.BlockSpec((tm, w), lambda i: (i, 0))
    bf = jax.ShapeDtypeStruct((T, D), BF16)
    f32 = jax.ShapeDtypeStruct((T, D), F32)
    return pl.pallas_call(
        _post_rec_proj_kernel,
        grid=(T // tm,),
        in_specs=[tok(D), tok(D), tok(PLE_DIM), const((D, D)), const((1, D)), const((D, D)),
                  const((PLE_DIM, D)), const((1, D)), const((D, D)), const((D, D)), const((D, D)),
                  const((D, D))],
        out_specs=[tok(D)] * 5,
        out_shape=[f32, bf, f32, bf, bf],
        compiler_params=pltpu.CompilerParams(
            dimension_semantics=("arbitrary",), vmem_limit_bytes=VMEM_LIMIT),
    )(z, h, p, wout, gain, wgate, wproj, gain1, wq, wf, wi, wg)


def _rec_levels():
    h = REC_CHUNK // 4
    out = []
    while h >= 1:
        out.append(h)
        h //= 2
    return out


def _rec_kernel(q_ref, fl_ref, v_ref, gs_ref, lb_ref, gain_ref, tril_ref, lvl_ref, z_ref,
                st_ref, b_scr, kb_scr):
    C = REC_CHUNK
    HALF = C // 2
    nq = C // REC_QUARTER

    @pl.when(pl.program_id(2) == 0)
    def _():
        st_ref[...] = jnp.zeros_like(st_ref)

    halves = (slice(0, HALF), slice(HALF, C))
    tril = tril_ref[...]

    def quarter_starts(b):
        ends = b.reshape(nq, REC_QUARTER, LANES)[:, REC_QUARTER - 1:, :]
        return jnp.concatenate([jnp.zeros_like(ends[:1]), ends[:-1]], axis=0), ends

    worst = jnp.zeros((1, LANES), F32)
    for hd in range(REC_HEADS_PER_STEP):
        lanes = slice(hd * REC_DIM, (hd + 1) * REC_DIM)
        lbs = lb_ref[:, lanes]
        e = jnp.exp(lbs - jnp.max(lbs, axis=0, keepdims=True))
        sm = e / jnp.sum(e, axis=0, keepdims=True)
        lb = (sm[0:1] + sm[1:2]) - sm[0:1]
        fl = fl_ref[0, :, lanes]
        log_f = jnp.log(lb + (1.0 - lb) * _sigmoid(fl))
        kb_scr[hd] = ((1.0 - lb) * _sigmoid(-fl)).astype(BF16)
        b = sum(_dot(tril, piece) for piece in _split_bf16(log_f, REC_PIECES)) * LOG2_E
        b_scr[hd] = b
        starts, ends = quarter_starts(b)
        worst = jnp.maximum(worst, jnp.max(starts - ends, axis=0))
    mild = jnp.max(worst) <= REC_MILD_DECAY

    def scores_and_output(hd, intra_quarter):
        lanes = slice(hd * REC_DIM, (hd + 1) * REC_DIM)
        b = b_scr[hd]
        qb = q_ref[0, :, lanes]
        kb = kb_scr[hd]
        v = v_ref[0, :, lanes]
        lvl = lvl_ref[...]

        def level_operands(ref):
            w = jnp.exp2(-jnp.abs(b - ref)).astype(BF16)
            return qb * w, kb * w

        def block_ref(h):
            b3 = b.reshape(C // (2 * h), 2 * h, LANES)
            return jnp.broadcast_to(b3[:, h - 1:h, :], b3.shape).reshape(C, LANES)

        levels = _rec_levels()
        if intra_quarter:
            starts, _ = quarter_starts(b)
            d = b - jnp.broadcast_to(starts, (nq, REC_QUARTER, LANES)).reshape(C, LANES)
            q_l, k_l = qb * jnp.exp2(d).astype(BF16), kb * jnp.exp2(-d).astype(BF16)
            inside = (lvl == 0) | (lvl > levels.index(REC_QUARTER) + 1)
            a_diag = [jnp.where(inside, _dot_nt(q_l[r], k_l[r]), 0.0) for r in halves]
            levels = [h for h in levels if h >= REC_QUARTER]
        else:
            a_diag = [jnp.where(lvl == 0, _dot_nt(qb[r], kb[r]), 0.0) for r in halves]
        row = lax.broadcasted_iota(jnp.int32, (C, LANES), 0)
        for h in levels:
            if h >= 4:
                ref = block_ref(h)
            elif h == 2:
                o4 = row & 3
                ref = jnp.where(o4 == 0, pltpu.roll(b, C - 1, 0),
                                jnp.where(o4 == 1, b, jnp.where(o4 == 2, pltpu.roll(b, 1, 0),
                                                                pltpu.roll(b, 2, 0))))
            else:
                ref = jnp.where((row & 1) == 0, b, pltpu.roll(b, 1, 0))
            q_l, k_l = level_operands(ref)
            own = lvl == _rec_levels().index(h) + 1
            a_diag = [jnp.where(own, _dot_nt(q_l[r], k_l[r]), a) for r, a in zip(halves, a_diag)]
        q_l, k_l = level_operands(b[HALF - 1:HALF, :])
        a_cross = _dot_nt(q_l[halves[1]], k_l[halves[0]])
        a = jnp.concatenate([jnp.concatenate([a_diag[0], jnp.zeros_like(a_cross)], axis=1),
                             jnp.concatenate([a_cross, a_diag[1]], axis=1)], axis=0)

        st = st_ref[hd]
        q_dec = qb * jnp.exp2(b).astype(BF16)
        o = _dot(a.astype(BF16), v) + _dot_nt(q_dec, st.astype(BF16))
        b_last = b[C - 1:C, :]
        k_dec = kb * jnp.exp2(b_last - b).astype(BF16)
        st_ref[hd] = st * jnp.exp2(b_last) + _dot_tn(v, k_dec)

        o = o * lax.rsqrt(jnp.mean(o * o, axis=-1, keepdims=True) + EPS) * gain_ref[...]
        z_ref[0, :, lanes] = (o * gs_ref[0, :, lanes].astype(F32)).astype(BF16)

    @pl.when(mild)
    def _():
        for hd in range(REC_HEADS_PER_STEP):
            scores_and_output(hd, intra_quarter=True)

    @pl.when(jnp.logical_not(mild))
    def _():
        for hd in range(REC_HEADS_PER_STEP):
            scores_and_output(hd, intra_quarter=False)


def _rec_level_map():
    size = REC_CHUNK // 2
    t = np.arange(size)[:, None]
    s = np.arange(size)[None, :]
    out = np.full((size, size), -1, np.int32)
    out[t == s] = 0
    for n, h in enumerate(_rec_levels()):
        own = (t // (2 * h) == s // (2 * h)) & ((t & h) != 0) & ((s & h) == 0)
        out[own] = n + 1
    return out


def _rec(q, fl, v, gs, rec_lb, gain, tril, lvl):
    B, S, D = q.shape
    C = REC_CHUNK
    width = REC_HEADS_PER_STEP * REC_DIM
    tok = pl.BlockSpec((1, C, width), lambda b, h, i: (b, i, h))
    const = lambda shape: pl.BlockSpec(shape, lambda b, h, i: (0,) * len(shape))
    return pl.pallas_call(
        _rec_kernel,
        grid=(B, REC_HEADS // REC_HEADS_PER_STEP, S // C),
        in_specs=[tok, tok, tok, tok,
                  pl.BlockSpec((rec_lb.shape[0], width), lambda b, h, i: (0, h)),
                  const((1, REC_DIM)), const((C, C)), const((C // 2, C // 2))],
        out_specs=tok,
        out_shape=jax.ShapeDtypeStruct((B, S, D), BF16),
        scratch_shapes=[pltpu.VMEM((REC_HEADS_PER_STEP, REC_DIM, REC_DIM), F32),
                        pltpu.VMEM((REC_HEADS_PER_STEP, C, REC_DIM), F32),
                        pltpu.VMEM((REC_HEADS_PER_STEP, C, REC_DIM), BF16)],
        compiler_params=pltpu.CompilerParams(
            dimension_semantics=("arbitrary", "arbitrary", "arbitrary"), vmem_limit_bytes=VMEM_LIMIT),
    )(q, fl, v, gs, rec_lb, gain, tril, lvl)


def _bias_lane(head, piece):
    return BIAS_LANES_PER_HEAD * head + piece


def _bias_selector():
    sel = np.zeros((C_PIECES, LANES, LANES), np.float32)
    for head in range(ATT_HEADS):
        for i in range(C_PIECES):
            sel[i, head, _bias_lane(head, i)] = 1.0
    return sel


def kernel(x, p, norm_pre, norm_post, att_w_in, att_b_f, att_w_out, rec_w_in, rec_lb, rec_out_norm,
           rec_w_out, ple_w_proj, ple_w_gate):
    B, S, D = x.shape
    assert D == D_MODEL and ROW_TILE == ATT_BLOCK and Q_TILES == 2
    assert S % (Q_GROUP * ATT_BLOCK) == 0 and S % REC_CHUNK == 0 and S // ATT_BLOCK <= LANES
    assert norm_pre.shape[0] == 2 and rec_lb.shape[0] == 2
    T = B * S
    W = ATT_HEADS * ATT_HEAD_DIM

    w_in = att_w_in[0]
    wq, wk, wv = (w_in[:, n * W:(n + 1) * W].astype(BF16) for n in range(3))
    wgf = jnp.pad(w_in[:, 3 * W:], ((0, 0), (0, LANES - ATT_HEADS))).astype(BF16)
    bf = jnp.pad(att_b_f[0], (0, LANES - ATT_HEADS)).reshape(1, LANES)
    tril_row = jnp.asarray(np.tril(np.ones((ROW_TILE, ROW_TILE), np.float32)), BF16)
    sel = jnp.asarray(_bias_selector(), BF16)
    qt, k, vt, gs, caug, nblk = _fox_proj(x, norm_pre[0:1], wq, wk, wv, wgf, bf, tril_row, sel)
    nblk = jnp.transpose(nblk[:, :, :, 0], (0, 2, 1)).reshape(-1)
    z = _fox_attn(nblk, qt, k, caug, vt, gs)
    r_in = rec_w_in[0]
    RW = REC_HEADS * REC_DIM
    rq, rf, ri, rg = (r_in[:, n * RW:(n + 1) * RW].astype(BF16) for n in range(4))
    h, q1, fl1, v1, gs1 = _post_rec_proj(
        z.reshape(T, D), x.reshape(T, D), p[0].reshape(T, PLE_DIM), att_w_out[0].astype(BF16),
        norm_post[0:1], ple_w_gate[0].astype(BF16), ple_w_proj[0].astype(BF16),
        norm_pre[1:2], rq, rf, ri, rg)
    tril_rec = jnp.asarray(np.tril(np.ones((REC_CHUNK, REC_CHUNK), np.float32)), BF16)
    lvl = jnp.asarray(_rec_level_map())
    shape3 = lambda a: a.reshape(B, S, D)
    z1 = _rec(shape3(q1), shape3(fl1), shape3(v1), shape3(gs1), rec_lb, rec_out_norm[0:1],
              tril_rec, lvl)
    out = _post(z1.reshape(T, D), h, p[1].reshape(T, PLE_DIM), rec_w_out[0].astype(BF16),
                norm_post[1:2], ple_w_gate[1].astype(BF16), ple_w_proj[1].astype(BF16))
    return out.reshape(B, S, D)
```

```python
import numpy as np
import jax
import jax.numpy as jnp
from jax import lax
from jax.experimental import pallas as pl
from jax.experimental.pallas import tpu as pltpu

F32 = jnp.float32
BF16 = jnp.bfloat16

D_MODEL = 1024
PLE_DIM = 256
ATT_HEADS = 16
ATT_HEAD_DIM = 64
REC_HEADS = 8
REC_DIM = 128
EPS = 1e-6
LOG2_E = 1.4426950408889634
SKIP_LOG2 = 128.0
NORM_MARGIN = 1.01
DOT_MARGIN = 2.0 ** -7

LANES = 128
HEAD_PAIRS = ATT_HEADS // 2
C_PIECES = 3
BIAS_LANES_PER_HEAD = 4

ROW_TILE = 512
ATT_BLOCK = 512
Q_TILES = 2
Q_GROUP = 8
COL_TILE = 256
M_INIT = -1e30
REC_CHUNK = 256
REC_HEADS_PER_STEP = 8
REC_PIECES = 2
REC_QUARTER = 64
REC_MILD_DECAY = 100.0
VMEM_LIMIT = 56 * 1024 * 1024

NT_DIMS = (((1,), (1,)), ((), ()))
TN_DIMS = (((0,), (0,)), ((), ()))


def _dot(a, b):
    return jnp.dot(a, b, preferred_element_type=F32)


def _dot_nt(a, b):
    return lax.dot_general(a, b, NT_DIMS, preferred_element_type=F32)


def _dot_tn(a, b):
    return lax.dot_general(a, b, TN_DIMS, preferred_element_type=F32)


def _split_bf16(x, pieces):
    out = []
    r = x
    for _ in range(pieces - 1):
        p = r.astype(BF16)
        out.append(p)
        r = r - p.astype(F32)
    out.append(r.astype(BF16))
    return out


def _rms_norm(x, gain):
    ms = jnp.mean(x * x, axis=-1, keepdims=True)
    return x * lax.rsqrt(ms + EPS) * gain


def _sigmoid(x):
    return 1.0 / (1.0 + jnp.exp(-x))


def _fox_proj_kernel(x_ref, gain_ref, wq_ref, wk_ref, wv_ref, wgf_ref, bf_ref,
                     tril_ref, sel_ref, qt_ref, k_ref, vt_ref, gs_ref, caug_ref, nblk_ref,
                     carry_ref, kmax_tab_ref, clast_tab_ref):
    tm = x_ref.shape[1]
    i = pl.program_id(1)

    @pl.when(i == 0)
    def _():
        carry_ref[...] = jnp.zeros_like(carry_ref)
        kmax_tab_ref[...] = jnp.zeros_like(kmax_tab_ref)
        clast_tab_ref[...] = jnp.zeros_like(clast_tab_ref)

    u = _rms_norm(x_ref[0], gain_ref[...]).astype(BF16)

    qb = (_dot(u, wq_ref[...]) * (ATT_HEAD_DIM ** -0.5 * LOG2_E)).astype(BF16)
    kb = _dot(u, wk_ref[...]).astype(BF16)
    qt = qb.T
    qt_ref[0, 0] = qt
    k_ref[0] = kb
    qtf, ktf = qt.astype(F32), kb.T.astype(F32)
    per_head = lambda t: jnp.sum(t.reshape(ATT_HEADS, ATT_HEAD_DIM, tm), axis=1)
    qn2, kn2, qk = per_head(qtf * qtf), per_head(ktf * ktf), per_head(qtf * ktf)
    vt_ref[0, 0] = _dot(u, wv_ref[...]).astype(BF16).T
    gf = _dot(u, wgf_ref[...])
    g = gf[:, :D_MODEL]
    gs_ref[0] = (g * _sigmoid(g)).astype(BF16)

    fl = gf[:, D_MODEL:] + bf_ref[...]
    log_f = jnp.minimum(fl, 0.0) - jnp.log1p(jnp.exp(-jnp.abs(fl)))
    tril = tril_ref[...]
    c = sum(_dot(tril, piece) for piece in _split_bf16(log_f, C_PIECES)) + carry_ref[...]
    carry_ref[...] = c[tm - 1:tm, :]
    c2 = c * LOG2_E
    caug = sum(_dot(piece, sel_ref[n]) for n, piece in enumerate(_split_bf16(-c2, C_PIECES)))
    caug_ref[0] = caug.astype(BF16)

    c2_t = c2.T[:ATT_HEADS]
    qmax = jnp.sqrt(jnp.max(qn2, axis=1, keepdims=True)) * NORM_MARGIN
    kmax = jnp.sqrt(jnp.max(kn2, axis=1, keepdims=True)) * NORM_MARGIN
    m_low = jnp.min(qk - c2_t, axis=1, keepdims=True) - DOT_MARGIN * qmax * kmax
    blk = lax.broadcasted_iota(jnp.int32, kmax_tab_ref.shape, 1)
    kmax_tab = jnp.where(blk == i, kmax, kmax_tab_ref[...])
    clast_tab = jnp.where(blk == i, c2_t[:, tm - 1:tm], clast_tab_ref[...])
    kmax_tab_ref[...] = kmax_tab
    clast_tab_ref[...] = clast_tab
    bound = qmax * kmax_tab - clast_tab - m_low
    first_kept = jnp.min(jnp.where((blk < i) & (bound >= -SKIP_LOG2), blk, i), axis=1, keepdims=True)
    nblk_ref[0, 0] = jnp.broadcast_to(i - first_kept + 1, nblk_ref.shape[2:])


def _fox_proj(x, gain, wq, wk, wv, wgf, bf, tril, sel):
    B, S, D = x.shape
    tm = ROW_TILE
    nb = S // tm
    const = lambda shape: pl.BlockSpec(shape, lambda b, i: (0,) * len(shape))
    tok = pl.BlockSpec((1, tm, D), lambda b, i: (b, i, 0))
    out_tok = jax.ShapeDtypeStruct((B, S, D), BF16)
    tiled_t = pl.BlockSpec((1, 1, D, tm), lambda b, i: (b, i, 0, 0))
    out_t = jax.ShapeDtypeStruct((B, nb, D, tm), BF16)
    return pl.pallas_call(
        _fox_proj_kernel,
        grid=(B, nb),
        in_specs=[tok, const((1, D)), const((D, D)), const((D, D)), const((D, D)), const((D, D + LANES)),
                  const((1, LANES)), const((tm, tm)), const((C_PIECES, LANES, LANES))],
        out_specs=[tiled_t, tok, tiled_t, tok, pl.BlockSpec((1, tm, LANES), lambda b, i: (b, i, 0)),
                   pl.BlockSpec((1, 1, ATT_HEADS, LANES), lambda b, i: (b, i, 0, 0))],
        out_shape=[out_t, out_tok, out_t, out_tok, jax.ShapeDtypeStruct((B, S, LANES), BF16),
                   jax.ShapeDtypeStruct((B, nb, ATT_HEADS, LANES), jnp.int32)],
        scratch_shapes=[pltpu.VMEM((1, LANES), F32), pltpu.VMEM((ATT_HEADS, LANES), F32),
                        pltpu.VMEM((ATT_HEADS, LANES), F32)],
        compiler_params=pltpu.CompilerParams(
            dimension_semantics=("arbitrary", "arbitrary"), vmem_limit_bytes=VMEM_LIMIT),
    )(x, gain, wq, wk, wv, wgf, bf, tril, sel)


def _fox_attn_kernel(nblk_ref, qt_ref, k_ref, caug_ref, vt_ref, gs_ref, z_ref, s_scr, mx_scr, m_scr, acc_scr):
    bk = ATT_BLOCK
    bq = Q_TILES * ATT_BLOCK
    b, pair, grp = pl.program_id(0), pl.program_id(1), pl.program_id(2)
    nb = pl.num_programs(2) * Q_GROUP
    ones_rows = jnp.ones((16, bk), BF16)
    bias_row = lax.broadcasted_iota(jnp.int32, (LANES, bq), 0)
    no_head = jnp.zeros((ATT_HEAD_DIM, bq), BF16)
    full = slice(0, bq)
    last_tile = slice(bq - bk, bq)

    def query_unit(sub, carry):
        unit = grp * (Q_GROUP // Q_TILES) + sub
        last = Q_TILES * unit + Q_TILES - 1
        q_t = jnp.concatenate([qt_ref[0, Q_TILES * sub + n] for n in range(Q_TILES)], axis=1)

        qt_aug, n_steps = [], []
        for hh in range(2):
            first_lane = BIAS_LANES_PER_HEAD * (2 * pair + hh)
            pick = jnp.where((bias_row >= first_lane) & (bias_row < first_lane + C_PIECES),
                             1.0, 0.0).astype(BF16)
            parts = [q_t[:ATT_HEAD_DIM], no_head] if hh == 0 else [no_head, q_t[ATT_HEAD_DIM:]]
            qt_aug.append(jnp.concatenate(parts + [pick], axis=0))
            head_row = (b * ATT_HEADS + 2 * pair + hh) * nb
            n = nblk_ref[head_row + last]
            for back in range(1, Q_TILES):
                n = jnp.maximum(n, nblk_ref[head_row + last - back] + back)
            n_steps.append(n)

        def put_scores(hh, j, buf, cols, diagonal):
            rows = pl.ds(pl.multiple_of(j * bk, bk), bk)
            k_aug = jnp.concatenate([k_ref[0, rows, :], caug_ref[0, rows, :]], axis=1)
            s_t = _dot(k_aug, qt_aug[hh][:, cols])
            if diagonal:
                shape = (bk, cols.stop - cols.start)
                key_pos = j * bk + lax.broadcasted_iota(jnp.int32, shape, 0)
                qry_pos = unit * bq + cols.start + lax.broadcasted_iota(jnp.int32, shape, 1)
                s_t = jnp.where(key_pos <= qry_pos, s_t, -jnp.inf)
            s_scr[hh, buf, :, cols] = s_t
            mx_scr[hh, buf, :, cols] = jnp.max(s_t, axis=0, keepdims=True)

        def update(hh, j, buf, cols):
            lo = hh * ATT_HEAD_DIM
            m = m_scr[hh, :, cols]
            m_new = jnp.maximum(m, mx_scr[hh, buf, :, cols])
            alpha = jnp.exp2(m - m_new)
            p_t = jnp.exp2((s_scr[hh, buf, :, cols] - m_new).astype(BF16))
            v_aug = jnp.concatenate([vt_ref[0, j, lo:lo + ATT_HEAD_DIM, :], ones_rows], axis=0)
            acc_scr[hh, :, cols] = alpha * acc_scr[hh, :, cols] + _dot(v_aug, p_t)
            m_scr[hh, :, cols] = m_new

        def run(*ops):
            split = [[slice(c, c + COL_TILE) for c in range(op[4].start, op[4].stop, COL_TILE)]
                     for op in ops]
            for n in range(max(len(tiles) for tiles in split)):
                for (kind, hh, j, buf, _, diagonal), tiles in zip(ops, split):
                    if n < len(tiles):
                        if kind == "put":
                            put_scores(hh, j, buf, tiles[n], diagonal)
                        else:
                            update(hh, j, buf, tiles[n])

        put = lambda hh, j, buf, cols=full, diagonal=False: ("put", hh, j, buf, cols, diagonal)
        upd = lambda hh, j, buf, cols=full: ("upd", hh, j, buf, cols, False)

        def step(hh, t, k, produce=True):
            ops = [put(hh, last - (t + 2), (3 + k) % 4)] if produce else []
            return ops + [upd(hh, last - t, (1 + k) % 4)]

        m_scr[...] = jnp.full(m_scr.shape, M_INIT, F32)
        acc_scr[...] = jnp.zeros(acc_scr.shape, F32)
        run(put(0, last, 0, last_tile, True), put(1, last, 0, last_tile, True))
        run(put(0, last - 1, 1, full, True), put(1, last - 1, 1, full, True))
        run(put(0, jnp.maximum(last - 2, 0), 2), upd(0, last, 0, last_tile),
            put(1, jnp.maximum(last - 2, 0), 2), upd(1, last, 0, last_tile))

        groups = [jnp.maximum(n - 3, 0) // 4 for n in n_steps]
        both = jnp.minimum(groups[0], groups[1])

        def four_steps_both(g, c):
            for n in range(4):
                run(*step(0, 1 + 4 * g + n, n), *step(1, 1 + 4 * g + n, n))
            return c

        lax.fori_loop(0, both, four_steps_both, 0)

        outs = []
        for hh in range(2):
            def four_steps(g, c, hh=hh):
                for n in range(4):
                    run(*step(hh, 1 + 4 * g + n, n))
                return c

            lax.fori_loop(both, groups[hh], four_steps, 0)
            t0 = 1 + 4 * groups[hh]
            rest = jnp.maximum(n_steps[hh] - 3, 0) - 4 * groups[hh]

            @pl.when(n_steps[hh] == 2)
            def _(hh=hh):
                run(upd(hh, last - 1, 1))

            for r in range(4):
                @pl.when((n_steps[hh] >= 3) & (rest == r))
                def _(r=r, hh=hh, t0=t0):
                    for n in range(r):
                        run(*step(hh, t0 + n, n))
                    run(*step(hh, t0 + r, r, produce=False), *step(hh, t0 + r + 1, r + 1, produce=False))

            acc = acc_scr[hh]
            outs.append(acc[:ATT_HEAD_DIM, :] / acc[ATT_HEAD_DIM:ATT_HEAD_DIM + 1, :])

        o = jnp.concatenate(outs, axis=0).T
        rows = pl.ds(pl.multiple_of(sub * bq, bq), bq)
        z_ref[0, rows, :] = (o * gs_ref[0, rows, :].astype(F32)).astype(BF16)
        return carry

    lax.fori_loop(0, Q_GROUP // Q_TILES, query_unit, 0)


def _fox_attn(nblk, qt, k, caug, vt, gs):
    B, S, D = k.shape
    bk = ATT_BLOCK
    bq = Q_TILES * ATT_BLOCK
    nb = S // bk
    tok = pl.BlockSpec((1, Q_GROUP * bk, LANES), lambda b, p, g, n: (b, g, p))
    seq = pl.BlockSpec((1, S, LANES), lambda b, p, g, n: (b, 0, p))
    return pl.pallas_call(
        _fox_attn_kernel,
        grid_spec=pltpu.PrefetchScalarGridSpec(
            num_scalar_prefetch=1,
            grid=(B, HEAD_PAIRS, nb // Q_GROUP),
            in_specs=[pl.BlockSpec((1, Q_GROUP, LANES, bk), lambda b, p, g, n: (b, g, p, 0)),
                      seq, pl.BlockSpec((1, S, LANES), lambda b, p, g, n: (b, 0, 0)),
                      pl.BlockSpec((1, nb, LANES, vt.shape[3]), lambda b, p, g, n: (b, 0, p, 0)),
                      tok],
            out_specs=tok,
            scratch_shapes=[pltpu.VMEM((2, 4, bk, bq), F32), pltpu.VMEM((2, 4, 1, bq), F32),
                            pltpu.VMEM((2, 1, bq), F32), pltpu.VMEM((2, ATT_HEAD_DIM + 16, bq), F32)]),
        out_shape=jax.ShapeDtypeStruct((B, S, D), BF16),
        compiler_params=pltpu.CompilerParams(
            dimension_semantics=("arbitrary", "arbitrary", "arbitrary"), vmem_limit_bytes=VMEM_LIMIT),
    )(nblk, qt, k, caug, vt, gs)


def _post_block(z_ref, h_ref, p_ref, wout_ref, gain_ref, wgate_ref, wproj_ref):
    y = _dot(z_ref[...], wout_ref[...])
    h1 = h_ref[...] + _rms_norm(y, gain_ref[...])
    gate = _sigmoid(_dot(h1.astype(BF16), wgate_ref[...]))
    return h1 + _dot(p_ref[...].astype(BF16), wproj_ref[...]) * gate


def _post_kernel(z_ref, h_ref, p_ref, wout_ref, gain_ref, wgate_ref, wproj_ref, o_ref):
    o_ref[...] = _post_block(z_ref, h_ref, p_ref, wout_ref, gain_ref, wgate_ref, wproj_ref)


def _post(z, h, p, wout, gain, wgate, wproj):
    T, D = h.shape
    tm = ROW_TILE
    const = lambda shape: pl.BlockSpec(shape, lambda i: (0,) * len(shape))
    tok = lambda w: pl.BlockSpec((tm, w), lambda i: (i, 0))
    return pl.pallas_call(
        _post_kernel,
        grid=(T // tm,),
        in_specs=[tok(D), tok(D), tok(PLE_DIM), const((D, D)), const((1, D)), const((D, D)),
                  const((PLE_DIM, D))],
        out_specs=tok(D),
        out_shape=jax.ShapeDtypeStruct((T, D), F32),
        compiler_params=pltpu.CompilerParams(
            dimension_semantics=("arbitrary",), vmem_limit_bytes=VMEM_LIMIT),
    )(z, h, p, wout, gain, wgate, wproj)


def _post_rec_proj_kernel(z_ref, h_ref, p_ref, wout_ref, gain_ref, wgate_ref, wproj_ref,
                          gain1_ref, wq_ref, wf_ref, wi_ref, wg_ref,
                          o_ref, q_ref, fl_ref, v_ref, gs_ref):
    h_new = _post_block(z_ref, h_ref, p_ref, wout_ref, gain_ref, wgate_ref, wproj_ref)
    o_ref[...] = h_new
    u = _rms_norm(h_new, gain1_ref[...]).astype(BF16)
    q_ref[...] = _dot(u, wq_ref[...]).astype(BF16)
    fl_ref[...] = _dot(u, wf_ref[...])
    v_ref[...] = _dot(u, wi_ref[...]).astype(BF16)
    g = _dot(u, wg_ref[...])
    gs_ref[...] = (g * _sigmoid(g)).astype(BF16)


def _post_rec_proj(z, h, p, wout, gain, wgate, wproj, gain1, wq, wf, wi, wg):
    T, D = h.shape
    tm = ROW_TILE
    const = lambda shape: pl.BlockSpec(shape, lambda i: (0,) * len(shape))
    tok = lambda w: pl.BlockSpec((tm, w), lambda i: (i, 0))
    bf = jax.ShapeDtypeStruct((T, D), BF16)
    f32 = jax.ShapeDtypeStruct((T, D), F32)
    return pl.pallas_call(
        _post_rec_proj_kernel,
        grid=(T // tm,),
        in_specs=[tok(D), tok(D), tok(PLE_DIM), const((D, D)), const((1, D)), const((D, D)),
                  const((PLE_DIM, D)), const((1, D)), const((D, D)), const((D, D)), const((D, D)),
                  const((D, D))],
        out_specs=[tok(D)] * 5,
        out_shape=[f32, bf, f32, bf, bf],
        compiler_params=pltpu.CompilerParams(
            dimension_semantics=("arbitrary",), vmem_limit_bytes=VMEM_LIMIT),
    )(z, h, p, wout, gain, wgate, wproj, gain1, wq, wf, wi, wg)


def _rec_levels():
    h = REC_CHUNK // 4
    out = []
    while h >= 1:
        out.append(h)
        h //= 2
    return out


def _rec_kernel(q_ref, fl_ref, v_ref, gs_ref, lb_ref, gain_ref, tril_ref, lvl_ref, z_ref,
                st_ref, b_scr, kb_scr):
    C = REC_CHUNK
    HALF = C // 2
    nq = C // REC_QUARTER

    @pl.when(pl.program_id(2) == 0)
    def _():
        st_ref[...] = jnp.zeros_like(st_ref)

    halves = (slice(0, HALF), slice(HALF, C))
    tril = tril_ref[...]

    def quarter_starts(b):
        ends = b.reshape(nq, REC_QUARTER, LANES)[:, REC_QUARTER - 1:, :]
        return jnp.concatenate([jnp.zeros_like(ends[:1]), ends[:-1]], axis=0), ends

    worst = jnp.zeros((1, LANES), F32)
    for hd in range(REC_HEADS_PER_STEP):
        lanes = slice(hd * REC_DIM, (hd + 1) * REC_DIM)
        lbs = lb_ref[:, lanes]
        e = jnp.exp(lbs - jnp.max(lbs, axis=0, keepdims=True))
        sm = e / jnp.sum(e, axis=0, keepdims=True)
        lb = (sm[0:1] + sm[1:2]) - sm[0:1]
        fl = fl_ref[0, :, lanes]
        log_f = jnp.log(lb + (1.0 - lb) * _sigmoid(fl))
        kb_scr[hd] = ((1.0 - lb) * _sigmoid(-fl)).astype(BF16)
        b = sum(_dot(tril, piece) for piece in _split_bf16(log_f, REC_PIECES)) * LOG2_E
        b_scr[hd] = b
        starts, ends = quarter_starts(b)
        worst = jnp.maximum(worst, jnp.max(starts - ends, axis=0))
    mild = jnp.max(worst) <= REC_MILD_DECAY

    def scores_and_output(hd, intra_quarter):
        lanes = slice(hd * REC_DIM, (hd + 1) * REC_DIM)
        b = b_scr[hd]
        qb = q_ref[0, :, lanes]
        kb = kb_scr[hd]
        v = v_ref[0, :, lanes]
        lvl = lvl_ref[...]

        def level_operands(ref):
            w = jnp.exp2(-jnp.abs(b - ref)).astype(BF16)
            return qb * w, kb * w

        def block_ref(h):
            b3 = b.reshape(C // (2 * h), 2 * h, LANES)
            return jnp.broadcast_to(b3[:, h - 1:h, :], b3.shape).reshape(C, LANES)

        levels = _rec_levels()
        if intra_quarter:
            starts, _ = quarter_starts(b)
            d = b - jnp.broadcast_to(starts, (nq, REC_QUARTER, LANES)).reshape(C, LANES)
            q_l, k_l = qb * jnp.exp2(d).astype(BF16), kb * jnp.exp2(-d).astype(BF16)
            inside = (lvl == 0) | (lvl > levels.index(REC_QUARTER) + 1)
            a_diag = [jnp.where(inside, _dot_nt(q_l[r], k_l[r]), 0.0) for r in halves]
            levels = [h for h in levels if h >= REC_QUARTER]
        else:
            a_diag = [jnp.where(lvl == 0, _dot_nt(qb[r], kb[r]), 0.0) for r in halves]
        yield
        row = lax.broadcasted_iota(jnp.int32, (C, LANES), 0)
        for h in levels:
            if h >= 4:
                ref = block_ref(h)
            elif h == 2:
                o4 = row & 3
                ref = jnp.where(o4 == 0, pltpu.roll(b, C - 1, 0),
                                jnp.where(o4 == 1, b, jnp.where(o4 == 2, pltpu.roll(b, 1, 0),
                                                                pltpu.roll(b, 2, 0))))
            else:
                ref = jnp.where((row & 1) == 0, b, pltpu.roll(b, 1, 0))
            q_l, k_l = level_operands(ref)
            own = lvl == _rec_levels().index(h) + 1
            a_diag = [jnp.where(own, _dot_nt(q_l[r], k_l[r]), a) for r, a in zip(halves, a_diag)]
            yield
        q_l, k_l = level_operands(b[HALF - 1:HALF, :])
        a_cross = _dot_nt(q_l[halves[1]], k_l[halves[0]])
        a = jnp.concatenate([jnp.concatenate([a_diag[0], jnp.zeros_like(a_cross)], axis=1),
                             jnp.concatenate([a_cross, a_diag[1]], axis=1)], axis=0)
        yield

        st = st_ref[hd]
        q_dec = qb * jnp.exp2(b).astype(BF16)
        o = _dot(a.astype(BF16), v) + _dot_nt(q_dec, st.astype(BF16))
        b_last = b[C - 1:C, :]
        k_dec = kb * jnp.exp2(b_last - b).astype(BF16)
        st_ref[hd] = st * jnp.exp2(b_last) + _dot_tn(v, k_dec)
        yield

        o = o * lax.rsqrt(jnp.mean(o * o, axis=-1, keepdims=True) + EPS) * gain_ref[...]
        z_ref[0, :, lanes] = (o * gs_ref[0, :, lanes].astype(F32)).astype(BF16)

    def all_heads(intra_quarter):
        live = [scores_and_output(hd, intra_quarter) for hd in range(REC_HEADS_PER_STEP)]
        while live:
            live = [stages for stages in live if next(stages, "done") != "done"]

    @pl.when(mild)
    def _():
        all_heads(intra_quarter=True)

    @pl.when(jnp.logical_not(mild))
    def _():
        all_heads(intra_quarter=False)


def _rec_level_map():
    size = REC_CHUNK // 2
    t = np.arange(size)[:, None]
    s = np.arange(size)[None, :]
    out = np.full((size, size), -1, np.int32)
    out[t == s] = 0
    for n, h in enumerate(_rec_levels()):
        own = (t // (2 * h) == s // (2 * h)) & ((t & h) != 0) & ((s & h) == 0)
        out[own] = n + 1
    return out


def _rec(q, fl, v, gs, rec_lb, gain, tril, lvl):
    B, S, D = q.shape
    C = REC_CHUNK
    width = REC_HEADS_PER_STEP * REC_DIM
    tok = pl.BlockSpec((1, C, width), lambda b, h, i: (b, i, h))
    const = lambda shape: pl.BlockSpec(shape, lambda b, h, i: (0,) * len(shape))
    return pl.pallas_call(
        _rec_kernel,
        grid=(B, REC_HEADS // REC_HEADS_PER_STEP, S // C),
        in_specs=[tok, tok, tok, tok,
                  pl.BlockSpec((rec_lb.shape[0], width), lambda b, h, i: (0, h)),
                  const((1, REC_DIM)), const((C, C)), const((C // 2, C // 2))],
        out_specs=tok,
        out_shape=jax.ShapeDtypeStruct((B, S, D), BF16),
        scratch_shapes=[pltpu.VMEM((REC_HEADS_PER_STEP, REC_DIM, REC_DIM), F32),
                        pltpu.VMEM((REC_HEADS_PER_STEP, C, REC_DIM), F32),
                        pltpu.VMEM((REC_HEADS_PER_STEP, C, REC_DIM), BF16)],
        compiler_params=pltpu.CompilerParams(
            dimension_semantics=("arbitrary", "arbitrary", "arbitrary"), vmem_limit_bytes=VMEM_LIMIT),
    )(q, fl, v, gs, rec_lb, gain, tril, lvl)


def _bias_lane(head, piece):
    return BIAS_LANES_PER_HEAD * head + piece


def _bias_selector():
    sel = np.zeros((C_PIECES, LANES, LANES), np.float32)
    for head in range(ATT_HEADS):
        for i in range(C_PIECES):
            sel[i, head, _bias_lane(head, i)] = 1.0
    return sel


def kernel(x, p, norm_pre, norm_post, att_w_in, att_b_f, att_w_out, rec_w_in, rec_lb, rec_out_norm,
           rec_w_out, ple_w_proj, ple_w_gate):
    B, S, D = x.shape
    assert D == D_MODEL and ROW_TILE == ATT_BLOCK and Q_TILES == 2
    assert S % (Q_GROUP * ATT_BLOCK) == 0 and S % REC_CHUNK == 0 and S // ATT_BLOCK <= LANES
    assert norm_pre.shape[0] == 2 and rec_lb.shape[0] == 2
    T = B * S
    W = ATT_HEADS * ATT_HEAD_DIM

    w_in = att_w_in[0]
    wq, wk, wv = (w_in[:, n * W:(n + 1) * W].astype(BF16) for n in range(3))
    wgf = jnp.pad(w_in[:, 3 * W:], ((0, 0), (0, LANES - ATT_HEADS))).astype(BF16)
    bf = jnp.pad(att_b_f[0], (0, LANES - ATT_HEADS)).reshape(1, LANES)
    tril_row = jnp.asarray(np.tril(np.ones((ROW_TILE, ROW_TILE), np.float32)), BF16)
    sel = jnp.asarray(_bias_selector(), BF16)
    qt, k, vt, gs, caug, nblk = _fox_proj(x, norm_pre[0:1], wq, wk, wv, wgf, bf, tril_row, sel)
    nblk = jnp.transpose(nblk[:, :, :, 0], (0, 2, 1)).reshape(-1)
    z = _fox_attn(nblk, qt, k, caug, vt, gs)
    r_in = rec_w_in[0]
    RW = REC_HEADS * REC_DIM
    rq, rf, ri, rg = (r_in[:, n * RW:(n + 1) * RW].astype(BF16) for n in range(4))
    h, q1, fl1, v1, gs1 = _post_rec_proj(
        z.reshape(T, D), x.reshape(T, D), p[0].reshape(T, PLE_DIM), att_w_out[0].astype(BF16),
        norm_post[0:1], ple_w_gate[0].astype(BF16), ple_w_proj[0].astype(BF16),
        norm_pre[1:2], rq, rf, ri, rg)
    tril_rec = jnp.asarray(np.tril(np.ones((REC_CHUNK, REC_CHUNK), np.float32)), BF16)
    lvl = jnp.asarray(_rec_level_map())
    shape3 = lambda a: a.reshape(B, S, D)
    z1 = _rec(shape3(q1), shape3(fl1), shape3(v1), shape3(gs1), rec_lb, rec_out_norm[0:1],
              tril_rec, lvl)
    out = _post(z1.reshape(T, D), h, p[1].reshape(T, PLE_DIM), rec_w_out[0].astype(BF16),
                norm_post[1:2], ple_w_gate[1].astype(BF16), ple_w_proj[1].astype(BF16))
    return out.reshape(B, S, D)
```

```python
import numpy as np
import jax
import jax.numpy as jnp
from jax import lax
from jax.experimental import pallas as pl
from jax.experimental.pallas import tpu as pltpu

F32 = jnp.float32
BF16 = jnp.bfloat16

D_MODEL = 1024
PLE_DIM = 256
ATT_HEADS = 16
ATT_HEAD_DIM = 64
REC_HEADS = 8
REC_DIM = 128
EPS = 1e-6
LOG2_E = 1.4426950408889634
SKIP_LOG2 = 128.0
NORM_MARGIN = 1.01
DOT_MARGIN = 2.0 ** -7

LANES = 128
HEAD_PAIRS = ATT_HEADS // 2
C_PIECES = 3
BIAS_LANES_PER_HEAD = 4

ROW_TILE = 512
ATT_BLOCK = 512
Q_TILES = 2
Q_GROUP = 8
COL_TILE = 256
M_INIT = -1e30
REC_CHUNK = 256
REC_HEADS_PER_STEP = 8
REC_PIECES = 2
REC_QUARTER = 64
REC_MILD_DECAY = 100.0
VMEM_LIMIT = 56 * 1024 * 1024

NT_DIMS = (((1,), (1,)), ((), ()))
TN_DIMS = (((0,), (0,)), ((), ()))


def _dot(a, b):
    return jnp.dot(a, b, preferred_element_type=F32)


def _dot_nt(a, b):
    return lax.dot_general(a, b, NT_DIMS, preferred_element_type=F32)


def _dot_tn(a, b):
    return lax.dot_general(a, b, TN_DIMS, preferred_element_type=F32)


def _split_bf16(x, pieces):
    out = []
    r = x
    for _ in range(pieces - 1):
        p = r.astype(BF16)
        out.append(p)
        r = r - p.astype(F32)
    out.append(r.astype(BF16))
    return out


def _rms_norm(x, gain):
    ms = jnp.mean(x * x, axis=-1, keepdims=True)
    return x * lax.rsqrt(ms + EPS) * gain


def _sigmoid(x):
    return 1.0 / (1.0 + jnp.exp(-x))


def _fox_proj_kernel(x_ref, gain_ref, wq_ref, wk_ref, wv_ref, wgf_ref, bf_ref,
                     tril_ref, sel_ref, qt_ref, k_ref, vt_ref, gs_ref, caug_ref, nblk_ref,
                     carry_ref, kmax_tab_ref, clast_tab_ref):
    tm = x_ref.shape[1]
    i = pl.program_id(1)

    @pl.when(i == 0)
    def _():
        carry_ref[...] = jnp.zeros_like(carry_ref)
        kmax_tab_ref[...] = jnp.zeros_like(kmax_tab_ref)
        clast_tab_ref[...] = jnp.zeros_like(clast_tab_ref)

    u = _rms_norm(x_ref[0], gain_ref[...]).astype(BF16)

    qb = (_dot(u, wq_ref[...]) * (ATT_HEAD_DIM ** -0.5 * LOG2_E)).astype(BF16)
    kb = _dot(u, wk_ref[...]).astype(BF16)
    qt = qb.T
    qt_ref[0, 0] = qt
    k_ref[0] = kb
    qtf, ktf = qt.astype(F32), kb.T.astype(F32)
    per_head = lambda t: jnp.sum(t.reshape(ATT_HEADS, ATT_HEAD_DIM, tm), axis=1)
    qn2, kn2, qk = per_head(qtf * qtf), per_head(ktf * ktf), per_head(qtf * ktf)
    vt_ref[0, 0] = _dot(u, wv_ref[...]).astype(BF16).T
    gf = _dot(u, wgf_ref[...])
    g = gf[:, :D_MODEL]
    gs_ref[0] = (g * _sigmoid(g)).astype(BF16)

    fl = gf[:, D_MODEL:] + bf_ref[...]
    log_f = jnp.minimum(fl, 0.0) - jnp.log1p(jnp.exp(-jnp.abs(fl)))
    tril = tril_ref[...]
    c = sum(_dot(tril, piece) for piece in _split_bf16(log_f, C_PIECES)) + carry_ref[...]
    carry_ref[...] = c[tm - 1:tm, :]
    c2 = c * LOG2_E
    caug = sum(_dot(piece, sel_ref[n]) for n, piece in enumerate(_split_bf16(-c2, C_PIECES)))
    caug_ref[0] = caug.astype(BF16)

    c2_t = c2.T[:ATT_HEADS]
    qmax = jnp.sqrt(jnp.max(qn2, axis=1, keepdims=True)) * NORM_MARGIN
    kmax = jnp.sqrt(jnp.max(kn2, axis=1, keepdims=True)) * NORM_MARGIN
    m_low = jnp.min(qk - c2_t, axis=1, keepdims=True) - DOT_MARGIN * qmax * kmax
    blk = lax.broadcasted_iota(jnp.int32, kmax_tab_ref.shape, 1)
    kmax_tab = jnp.where(blk == i, kmax, kmax_tab_ref[...])
    clast_tab = jnp.where(blk == i, c2_t[:, tm - 1:tm], clast_tab_ref[...])
    kmax_tab_ref[...] = kmax_tab
    clast_tab_ref[...] = clast_tab
    bound = qmax * kmax_tab - clast_tab - m_low
    first_kept = jnp.min(jnp.where((blk < i) & (bound >= -SKIP_LOG2), blk, i), axis=1, keepdims=True)
    nblk_ref[0, 0] = jnp.broadcast_to(i - first_kept + 1, nblk_ref.shape[2:])


def _fox_proj(x, gain, wq, wk, wv, wgf, bf, tril, sel):
    B, S, D = x.shape
    tm = ROW_TILE
    nb = S // tm
    const = lambda shape: pl.BlockSpec(shape, lambda b, i: (0,) * len(shape))
    tok = pl.BlockSpec((1, tm, D), lambda b, i: (b, i, 0))
    out_tok = jax.ShapeDtypeStruct((B, S, D), BF16)
    tiled_t = pl.BlockSpec((1, 1, D, tm), lambda b, i: (b, i, 0, 0))
    out_t = jax.ShapeDtypeStruct((B, nb, D, tm), BF16)
    return pl.pallas_call(
        _fox_proj_kernel,
        grid=(B, nb),
        in_specs=[tok, const((1, D)), const((D, D)), const((D, D)), const((D, D)), const((D, D + LANES)),
                  const((1, LANES)), const((tm, tm)), const((C_PIECES, LANES, LANES))],
        out_specs=[tiled_t, tok, tiled_t, tok, pl.BlockSpec((1, tm, LANES), lambda b, i: (b, i, 0)),
                   pl.BlockSpec((1, 1, ATT_HEADS, LANES), lambda b, i: (b, i, 0, 0))],
        out_shape=[out_t, out_tok, out_t, out_tok, jax.ShapeDtypeStruct((B, S, LANES), BF16),
                   jax.ShapeDtypeStruct((B, nb, ATT_HEADS, LANES), jnp.int32)],
        scratch_shapes=[pltpu.VMEM((1, LANES), F32), pltpu.VMEM((ATT_HEADS, LANES), F32),
                        pltpu.VMEM((ATT_HEADS, LANES), F32)],
        compiler_params=pltpu.CompilerParams(
            dimension_semantics=("arbitrary", "arbitrary"), vmem_limit_bytes=VMEM_LIMIT),
    )(x, gain, wq, wk, wv, wgf, bf, tril, sel)


def _fox_attn_kernel(nblk_ref, qt_ref, k_ref, caug_ref, vt_ref, gs_ref, z_ref, s_scr, mx_scr, m_scr, acc_scr):
    bk = ATT_BLOCK
    bq = Q_TILES * ATT_BLOCK
    b, pair, grp = pl.program_id(0), pl.program_id(1), pl.program_id(2)
    nb = pl.num_programs(2) * Q_GROUP
    ones_rows = jnp.ones((16, bk), BF16)
    bias_row = lax.broadcasted_iota(jnp.int32, (LANES, bq), 0)
    no_head = jnp.zeros((ATT_HEAD_DIM, bq), BF16)
    full = slice(0, bq)
    last_tile = slice(bq - bk, bq)

    def query_unit(sub, carry):
        unit = grp * (Q_GROUP // Q_TILES) + sub
        last = Q_TILES * unit + Q_TILES - 1
        q_t = jnp.concatenate([qt_ref[0, Q_TILES * sub + n] for n in range(Q_TILES)], axis=1)

        qt_aug, n_steps = [], []
        for hh in range(2):
            first_lane = BIAS_LANES_PER_HEAD * (2 * pair + hh)
            pick = jnp.where((bias_row >= first_lane) & (bias_row < first_lane + C_PIECES),
                             1.0, 0.0).astype(BF16)
            parts = [q_t[:ATT_HEAD_DIM], no_head] if hh == 0 else [no_head, q_t[ATT_HEAD_DIM:]]
            qt_aug.append(jnp.concatenate(parts + [pick], axis=0))
            head_row = (b * ATT_HEADS + 2 * pair + hh) * nb
            n = nblk_ref[head_row + last]
            for back in range(1, Q_TILES):
                n = jnp.maximum(n, nblk_ref[head_row + last - back] + back)
            n_steps.append(n)

        def put_scores(hh, j, buf, cols, diagonal):
            rows = pl.ds(pl.multiple_of(j * bk, bk), bk)
            k_aug = jnp.concatenate([k_ref[0, rows, :], caug_ref[0, rows, :]], axis=1)
            s_t = _dot(k_aug, qt_aug[hh][:, cols])
            if diagonal:
                shape = (bk, cols.stop - cols.start)
                key_pos = j * bk + lax.broadcasted_iota(jnp.int32, shape, 0)
                qry_pos = unit * bq + cols.start + lax.broadcasted_iota(jnp.int32, shape, 1)
                s_t = jnp.where(key_pos <= qry_pos, s_t, -jnp.inf)
            s_scr[hh, buf, :, cols] = s_t
            mx_scr[hh, buf, :, cols] = jnp.max(s_t, axis=0, keepdims=True)

        def update(hh, j, buf, cols):
            lo = hh * ATT_HEAD_DIM
            m = m_scr[hh, :, cols]
            m_new = jnp.maximum(m, mx_scr[hh, buf, :, cols])
            alpha = jnp.exp2(m - m_new)
            p_t = jnp.exp2((s_scr[hh, buf, :, cols] - m_new).astype(BF16))
            v_aug = jnp.concatenate([vt_ref[0, j, lo:lo + ATT_HEAD_DIM, :], ones_rows], axis=0)
            acc_scr[hh, :, cols] = alpha * acc_scr[hh, :, cols] + _dot(v_aug, p_t)
            m_scr[hh, :, cols] = m_new

        def run(*ops):
            split = [[slice(c, c + COL_TILE) for c in range(op[4].start, op[4].stop, COL_TILE)]
                     for op in ops]
            for n in range(max(len(tiles) for tiles in split)):
                for (kind, hh, j, buf, _, diagonal), tiles in zip(ops, split):
                    if n < len(tiles):
                        if kind == "put":
                            put_scores(hh, j, buf, tiles[n], diagonal)
                        else:
                            update(hh, j, buf, tiles[n])

        put = lambda hh, j, buf, cols=full, diagonal=False: ("put", hh, j, buf, cols, diagonal)
        upd = lambda hh, j, buf, cols=full: ("upd", hh, j, buf, cols, False)

        def step(hh, t, k, produce=True):
            ops = [put(hh, last - (t + 2), (3 + k) % 4)] if produce else []
            return ops + [upd(hh, last - t, (1 + k) % 4)]

        m_scr[...] = jnp.full(m_scr.shape, M_INIT, F32)
        acc_scr[...] = jnp.zeros(acc_scr.shape, F32)
        run(put(0, last, 0, last_tile, True), put(1, last, 0, last_tile, True))
        run(put(0, last - 1, 1, full, True), put(1, last - 1, 1, full, True))
        run(put(0, jnp.maximum(last - 2, 0), 2), upd(0, last, 0, last_tile),
            put(1, jnp.maximum(last - 2, 0), 2), upd(1, last, 0, last_tile))

        groups = [jnp.maximum(n - 3, 0) // 4 for n in n_steps]
        both = jnp.minimum(groups[0], groups[1])

        def four_steps_both(g, c):
            for n in range(4):
                run(*step(0, 1 + 4 * g + n, n), *step(1, 1 + 4 * g + n, n))
            return c

        lax.fori_loop(0, both, four_steps_both, 0)

        outs = []
        for hh in range(2):
            def four_steps(g, c, hh=hh):
                for n in range(4):
                    run(*step(hh, 1 + 4 * g + n, n))
                return c

            lax.fori_loop(both, groups[hh], four_steps, 0)
            t0 = 1 + 4 * groups[hh]
            rest = jnp.maximum(n_steps[hh] - 3, 0) - 4 * groups[hh]

            @pl.when(n_steps[hh] == 2)
            def _(hh=hh):
                run(upd(hh, last - 1, 1))

            for r in range(4):
                @pl.when((n_steps[hh] >= 3) & (rest == r))
                def _(r=r, hh=hh, t0=t0):
                    for n in range(r):
                        run(*step(hh, t0 + n, n))
                    run(*step(hh, t0 + r, r, produce=False), *step(hh, t0 + r + 1, r + 1, produce=False))

            acc = acc_scr[hh]
            outs.append(acc[:ATT_HEAD_DIM, :] / acc[ATT_HEAD_DIM:ATT_HEAD_DIM + 1, :])

        o = jnp.concatenate(outs, axis=0).T
        rows = pl.ds(pl.multiple_of(sub * bq, bq), bq)
        z_ref[0, rows, :] = (o * gs_ref[0, rows, :].astype(F32)).astype(BF16)
        return carry

    lax.fori_loop(0, Q_GROUP // Q_TILES, query_unit, 0)


def _fox_attn(nblk, qt, k, caug, vt, gs):
    B, S, D = k.shape
    bk = ATT_BLOCK
    bq = Q_TILES * ATT_BLOCK
    nb = S // bk
    tok = pl.BlockSpec((1, Q_GROUP * bk, LANES), lambda b, p, g, n: (b, g, p))
    seq = pl.BlockSpec((1, S, LANES), lambda b, p, g, n: (b, 0, p))
    return pl.pallas_call(
        _fox_attn_kernel,
        grid_spec=pltpu.PrefetchScalarGridSpec(
            num_scalar_prefetch=1,
            grid=(B, HEAD_PAIRS, nb // Q_GROUP),
            in_specs=[pl.BlockSpec((1, Q_GROUP, LANES, bk), lambda b, p, g, n: (b, g, p, 0)),
                      seq, pl.BlockSpec((1, S, LANES), lambda b, p, g, n: (b, 0, 0)),
                      pl.BlockSpec((1, nb, LANES, vt.shape[3]), lambda b, p, g, n: (b, 0, p, 0)),
                      tok],
            out_specs=tok,
            scratch_shapes=[pltpu.VMEM((2, 4, bk, bq), F32), pltpu.VMEM((2, 4, 1, bq), F32),
                            pltpu.VMEM((2, 1, bq), F32), pltpu.VMEM((2, ATT_HEAD_DIM + 16, bq), F32)]),
        out_shape=jax.ShapeDtypeStruct((B, S, D), BF16),
        compiler_params=pltpu.CompilerParams(
            dimension_semantics=("arbitrary", "arbitrary", "arbitrary"), vmem_limit_bytes=VMEM_LIMIT),
    )(nblk, qt, k, caug, vt, gs)


def _post_block(z_ref, h_ref, p_ref, wout_ref, gain_ref, wgate_ref, wproj_ref):
    y = _dot(z_ref[...], wout_ref[...])
    h1 = h_ref[...] + _rms_norm(y, gain_ref[...])
    gate = _sigmoid(_dot(h1.astype(BF16), wgate_ref[...]))
    return h1 + _dot(p_ref[...].astype(BF16), wproj_ref[...]) * gate


def _post_kernel(z_ref, h_ref, p_ref, wout_ref, gain_ref, wgate_ref, wproj_ref, o_ref):
    o_ref[...] = _post_block(z_ref, h_ref, p_ref, wout_ref, gain_ref, wgate_ref, wproj_ref)


def _post(z, h, p, wout, gain, wgate, wproj):
    T, D = h.shape
    tm = ROW_TILE
    const = lambda shape: pl.BlockSpec(shape, lambda i: (0,) * len(shape))
    tok = lambda w: pl.BlockSpec((tm, w), lambda i: (i, 0))
    return pl.pallas_call(
        _post_kernel,
        grid=(T // tm,),
        in_specs=[tok(D), tok(D), tok(PLE_DIM), const((D, D)), const((1, D)), const((D, D)),
                  const((PLE_DIM, D))],
        out_specs=tok(D),
        out_shape=jax.ShapeDtypeStruct((T, D), F32),
        compiler_params=pltpu.CompilerParams(
            dimension_semantics=("arbitrary",), vmem_limit_bytes=VMEM_LIMIT),
    )(z, h, p, wout, gain, wgate, wproj)


def _post_rec_proj_kernel(z_ref, h_ref, p_ref, wout_ref, gain_ref, wgate_ref, wproj_ref,
                          gain1_ref, wq_ref, wf_ref, wi_ref, wg_ref,
                          o_ref, q_ref, fl_ref, v_ref, gs_ref):
    h_new = _post_block(z_ref, h_ref, p_ref, wout_ref, gain_ref, wgate_ref, wproj_ref)
    o_ref[...] = h_new
    u = _rms_norm(h_new, gain1_ref[...]).astype(BF16)
    q_ref[...] = _dot(u, wq_ref[...]).astype(BF16)
    fl_ref[...] = _dot(u, wf_ref[...])
    v_ref[...] = _dot(u, wi_ref[...]).astype(BF16)
    g = _dot(u, wg_ref[...])
    gs_ref[...] = (g * _sigmoid(g)).astype(BF16)


def _post_rec_proj(z, h, p, wout, gain, wgate, wproj, gain1, wq, wf, wi, wg):
    T, D = h.shape
    tm = ROW_TILE
    const = lambda shape: pl.BlockSpec(shape, lambda i: (0,) * len(shape))
    tok = lambda w: pl.BlockSpec((tm, w), lambda i: (i, 0))
    bf = jax.ShapeDtypeStruct((T, D), BF16)
    f32 = jax.ShapeDtypeStruct((T, D), F32)
    return pl.pallas_call(
        _post_rec_proj_kernel,
        grid=(T // tm,),
        in_specs=[tok(D), tok(D), tok(PLE_DIM), const((D, D)), const((1, D)), const((D, D)),
                  const((PLE_DIM, D)), const((1, D)), const((D, D)), const((D, D)), const((D, D)),
                  const((D, D))],
        out_specs=[tok(D)] * 5,
        out_shape=[f32, bf, f32, bf, bf],
        compiler_params=pltpu.CompilerParams(
            dimension_semantics=("arbitrary",), vmem_limit_bytes=VMEM_LIMIT),
    )(z, h, p, wout, gain, wgate, wproj, gain1, wq, wf, wi, wg)


def _rec_levels():
    h = REC_CHUNK // 4
    out = []
    while h >= 1:
        out.append(h)
        h //= 2
    return out


def _rec_kernel(q_ref, fl_ref, v_ref, gs_ref, lb_ref, gain_ref, tril_ref, lvl_ref, z_ref,
                st_ref, b_scr, kb_scr):
    C = REC_CHUNK
    HALF = C // 2
    nq = C // REC_QUARTER

    @pl.when(pl.program_id(2) == 0)
    def _():
        st_ref[...] = jnp.zeros_like(st_ref)

    halves = (slice(0, HALF), slice(HALF, C))
    tril = tril_ref[...]

    def quarter_starts(b):
        ends = b.reshape(nq, REC_QUARTER, LANES)[:, REC_QUARTER - 1:, :]
        return jnp.concatenate([jnp.zeros_like(ends[:1]), ends[:-1]], axis=0), ends

    worst = jnp.zeros((1, LANES), F32)
    for hd in range(REC_HEADS_PER_STEP):
        lanes = slice(hd * REC_DIM, (hd + 1) * REC_DIM)
        lbs = lb_ref[:, lanes]
        e = jnp.exp(lbs - jnp.max(lbs, axis=0, keepdims=True))
        sm = e / jnp.sum(e, axis=0, keepdims=True)
        lb = (sm[0:1] + sm[1:2]) - sm[0:1]
        fl = fl_ref[0, :, lanes]
        log_f = jnp.log(lb + (1.0 - lb) * _sigmoid(fl))
        kb_scr[hd] = ((1.0 - lb) * _sigmoid(-fl)).astype(BF16)
        b = sum(_dot(tril, piece) for piece in _split_bf16(log_f, REC_PIECES)) * LOG2_E
        b_scr[hd] = b
        starts, ends = quarter_starts(b)
        worst = jnp.maximum(worst, jnp.max(starts - ends, axis=0))
    mild = jnp.max(worst) <= REC_MILD_DECAY

    def scores_and_output(hd, intra_quarter):
        lanes = slice(hd * REC_DIM, (hd + 1) * REC_DIM)
        b = b_scr[hd]
        qb = q_ref[0, :, lanes]
        kb = kb_scr[hd]
        v = v_ref[0, :, lanes]
        lvl = lvl_ref[...]

        def level_operands(ref):
            w = jnp.exp2(-jnp.abs(b - ref)).astype(BF16)
            return qb * w, kb * w

        def block_ref(h):
            b3 = b.reshape(C // (2 * h), 2 * h, LANES)
            return jnp.broadcast_to(b3[:, h - 1:h, :], b3.shape).reshape(C, LANES)

        levels = _rec_levels()
        if intra_quarter:
            starts, _ = quarter_starts(b)
            d = b - jnp.broadcast_to(starts, (nq, REC_QUARTER, LANES)).reshape(C, LANES)
            q_l, k_l = qb * jnp.exp2(d).astype(BF16), kb * jnp.exp2(-d).astype(BF16)
            inside = (lvl == 0) | (lvl > levels.index(REC_QUARTER) + 1)
            a_diag = [jnp.where(inside, _dot_nt(q_l[r], k_l[r]), 0.0) for r in halves]
            levels = [h for h in levels if h >= REC_QUARTER]
        else:
            a_diag = [jnp.where(lvl == 0, _dot_nt(qb[r], kb[r]), 0.0) for r in halves]
        yield
        row = lax.broadcasted_iota(jnp.int32, (C, LANES), 0)
        for h in levels:
            if h >= 4:
                ref = block_ref(h)
            elif h == 2:
                o4 = row & 3
                ref = jnp.where(o4 == 0, pltpu.roll(b, C - 1, 0),
                                jnp.where(o4 == 1, b, jnp.where(o4 == 2, pltpu.roll(b, 1, 0),
                                                                pltpu.roll(b, 2, 0))))
            else:
                ref = jnp.where((row & 1) == 0, b, pltpu.roll(b, 1, 0))
            q_l, k_l = level_operands(ref)
            yield
            own = lvl == _rec_levels().index(h) + 1
            a_diag = [jnp.where(own, _dot_nt(q_l[r], k_l[r]), a) for r, a in zip(halves, a_diag)]
            yield
        q_l, k_l = level_operands(b[HALF - 1:HALF, :])
        a_cross = _dot_nt(q_l[halves[1]], k_l[halves[0]])
        a = jnp.concatenate([jnp.concatenate([a_diag[0], jnp.zeros_like(a_cross)], axis=1),
                             jnp.concatenate([a_cross, a_diag[1]], axis=1)], axis=0)
        yield

        st = st_ref[hd]
        q_dec = qb * jnp.exp2(b).astype(BF16)
        o = _dot(a.astype(BF16), v) + _dot_nt(q_dec, st.astype(BF16))
        yield
        b_last = b[C - 1:C, :]
        k_dec = kb * jnp.exp2(b_last - b).astype(BF16)
        st_ref[hd] = st * jnp.exp2(b_last) + _dot_tn(v, k_dec)
        yield

        o = o * lax.rsqrt(jnp.mean(o * o, axis=-1, keepdims=True) + EPS) * gain_ref[...]
        z_ref[0, :, lanes] = (o * gs_ref[0, :, lanes].astype(F32)).astype(BF16)

    def all_heads(intra_quarter):
        live = [scores_and_output(hd, intra_quarter) for hd in range(REC_HEADS_PER_STEP)]
        while live:
            live = [stages for stages in live if next(stages, "done") != "done"]

    @pl.when(mild)
    def _():
        all_heads(intra_quarter=True)

    @pl.when(jnp.logical_not(mild))
    def _():
        all_heads(intra_quarter=False)


def _rec_level_map():
    size = REC_CHUNK // 2
    t = np.arange(size)[:, None]
    s = np.arange(size)[None, :]
    out = np.full((size, size), -1, np.int32)
    out[t == s] = 0
    for n, h in enumerate(_rec_levels()):
        own = (t // (2 * h) == s // (2 * h)) & ((t & h) != 0) & ((s & h) == 0)
        out[own] = n + 1
    return out


def _rec(q, fl, v, gs, rec_lb, gain, tril, lvl):
    B, S, D = q.shape
    C = REC_CHUNK
    width = REC_HEADS_PER_STEP * REC_DIM
    tok = pl.BlockSpec((1, C, width), lambda b, h, i: (b, i, h))
    const = lambda shape: pl.BlockSpec(shape, lambda b, h, i: (0,) * len(shape))
    return pl.pallas_call(
        _rec_kernel,
        grid=(B, REC_HEADS // REC_HEADS_PER_STEP, S // C),
        in_specs=[tok, tok, tok, tok,
                  pl.BlockSpec((rec_lb.shape[0], width), lambda b, h, i: (0, h)),
                  const((1, REC_DIM)), const((C, C)), const((C // 2, C // 2))],
        out_specs=tok,
        out_shape=jax.ShapeDtypeStruct((B, S, D), BF16),
        scratch_shapes=[pltpu.VMEM((REC_HEADS_PER_STEP, REC_DIM, REC_DIM), F32),
                        pltpu.VMEM((REC_HEADS_PER_STEP, C, REC_DIM), F32),
                        pltpu.VMEM((REC_HEADS_PER_STEP, C, REC_DIM), BF16)],
        compiler_params=pltpu.CompilerParams(
            dimension_semantics=("arbitrary", "arbitrary", "arbitrary"), vmem_limit_bytes=VMEM_LIMIT),
    )(q, fl, v, gs, rec_lb, gain, tril, lvl)


def _bias_lane(head, piece):
    return BIAS_LANES_PER_HEAD * head + piece


def _bias_selector():
    sel = np.zeros((C_PIECES, LANES, LANES), np.float32)
    for head in range(ATT_HEADS):
        for i in range(C_PIECES):
            sel[i, head, _bias_lane(head, i)] = 1.0
    return sel


def kernel(x, p, norm_pre, norm_post, att_w_in, att_b_f, att_w_out, rec_w_in, rec_lb, rec_out_norm,
           rec_w_out, ple_w_proj, ple_w_gate):
    B, S, D = x.shape
    assert D == D_MODEL and ROW_TILE == ATT_BLOCK and Q_TILES == 2
    assert S % (Q_GROUP * ATT_BLOCK) == 0 and S % REC_CHUNK == 0 and S // ATT_BLOCK <= LANES
    assert norm_pre.shape[0] == 2 and rec_lb.shape[0] == 2
    T = B * S
    W = ATT_HEADS * ATT_HEAD_DIM

    w_in = att_w_in[0]
    wq, wk, wv = (w_in[:, n * W:(n + 1) * W].astype(BF16) for n in range(3))
    wgf = jnp.pad(w_in[:, 3 * W:], ((0, 0), (0, LANES - ATT_HEADS))).astype(BF16)
    bf = jnp.pad(att_b_f[0], (0, LANES - ATT_HEADS)).reshape(1, LANES)
    tril_row = jnp.asarray(np.tril(np.ones((ROW_TILE, ROW_TILE), np.float32)), BF16)
    sel = jnp.asarray(_bias_selector(), BF16)
    qt, k, vt, gs, caug, nblk = _fox_proj(x, norm_pre[0:1], wq, wk, wv, wgf, bf, tril_row, sel)
    nblk = jnp.transpose(nblk[:, :, :, 0], (0, 2, 1)).reshape(-1)
    z = _fox_attn(nblk, qt, k, caug, vt, gs)
    r_in = rec_w_in[0]
    RW = REC_HEADS * REC_DIM
    rq, rf, ri, rg = (r_in[:, n * RW:(n + 1) * RW].astype(BF16) for n in range(4))
    h, q1, fl1, v1, gs1 = _post_rec_proj(
        z.reshape(T, D), x.reshape(T, D), p[0].reshape(T, PLE_DIM), att_w_out[0].astype(BF16),
        norm_post[0:1], ple_w_gate[0].astype(BF16), ple_w_proj[0].astype(BF16),
        norm_pre[1:2], rq, rf, ri, rg)
    tril_rec = jnp.asarray(np.tril(np.ones((REC_CHUNK, REC_CHUNK), np.float32)), BF16)
    lvl = jnp.asarray(_rec_level_map())
    shape3 = lambda a: a.reshape(B, S, D)
    z1 = _rec(shape3(q1), shape3(fl1), shape3(v1), shape3(gs1), rec_lb, rec_out_norm[0:1],
              tril_rec, lvl)
    out = _post(z1.reshape(T, D), h, p[1].reshape(T, PLE_DIM), rec_w_out[0].astype(BF16),
                norm_post[1:2], ple_w_gate[1].astype(BF16), ple_w_proj[1].astype(BF16))
    return out.reshape(B, S, D)
```
